```python
import math
import jax, jax.numpy as jnp
from jax import lax
import numpy as np

D_MODEL = 1024
BATCH = 16
SEQ = 2048
DEPTH = 4

MEM_LEN = 256

SSD_HEAD_DIM = 64
SSD_INNER = D_MODEL
SSD_HEADS = SSD_INNER // SSD_HEAD_DIM
SSD_GROUPS = 2
SSD_STATE = 64
SSD_CONV = 4
SSD_CHUNK = 128
SSD_CONV_DIM = SSD_INNER + 2 * SSD_GROUPS * SSD_STATE

SB_HEADS = 8
SB_HEAD_DIM = 64
SB_WIDTH = SB_HEADS * SB_HEAD_DIM
SB_BLOCK = 128

XA_HEADS = 4
XA_HEAD_DIM = 128
XA_WIDTH = XA_HEADS * XA_HEAD_DIM

D_FF = 2816
N_EXPERTS = 8
TOP_K = 2
N_DENSE = (DEPTH + 1) // 2
N_MOE = DEPTH // 2

IN_WIDTH = SSD_INNER + SSD_CONV_DIM + SSD_HEADS + 3 * SB_WIDTH + 2 * D_MODEL
EPS = 1e-6

kernel_name = "hybrid_ssd_stickbreaking_moe_trunk"


def rmsnorm(x, g):
    xf = x.astype(jnp.float32)
    y = xf * lax.rsqrt(jnp.mean(xf * xf, axis=-1, keepdims=True) + EPS)
    return (y * g.astype(jnp.float32)).astype(x.dtype)


def causal_dwconv(x, w, b):
    S = x.shape[1]
    K = w.shape[0]
    xp = jnp.pad(x, ((0, 0), (K - 1, 0), (0, 0)))
    y = b
    for k in range(K):
        y = y + xp[:, k:k + S] * w[k]
    return y


def ssd_branch(z, xbc, dt_raw, conv_w, conv_b, dt_bias, a_log, d_skip, norm_g):
    Bsz, S, _ = xbc.shape
    G, hg, P, N, L = SSD_GROUPS, SSD_HEADS // SSD_GROUPS, SSD_HEAD_DIM, SSD_STATE, SSD_CHUNK
    nc = S // L
    xbc = jax.nn.silu(causal_dwconv(xbc, conv_w, conv_b))
    xs, Bm, Cm = jnp.split(xbc, [SSD_INNER, SSD_INNER + G * N], axis=-1)
    x = xs.reshape(Bsz, nc, L, G, hg, P)
    Bm = Bm.reshape(Bsz, nc, L, G, N)
    Cm = Cm.reshape(Bsz, nc, L, G, N)
    dt = jax.nn.softplus((dt_raw + dt_bias).astype(jnp.float32))
    a = (dt * -jnp.exp(a_log.astype(jnp.float32))).reshape(Bsz, nc, L, G, hg)
    xdt = x * dt.reshape(Bsz, nc, L, G, hg)[..., None].astype(x.dtype)
    a_cum = jnp.cumsum(a, axis=2)
    causal = jnp.tril(jnp.ones((L, L), dtype=bool))
    seg = a_cum[:, :, :, None] - a_cum[:, :, None, :]
    decay = jnp.exp(jnp.where(causal[:, :, None, None], seg, -jnp.inf))
    cb = jnp.einsum('bctgn,bcsgn->bctsg', Cm, Bm)
    y_diag = jnp.einsum('bctsgh,bcsghp->bctghp', (cb[..., None] * decay).astype(x.dtype), xdt)
    decay_to_end = jnp.exp(a_cum[:, :, -1:] - a_cum)
    states = jnp.einsum('bclgn,bclghp->bcghpn', Bm, xdt * decay_to_end[..., None].astype(x.dtype))
    chunk_decay = jnp.exp(a_cum[:, :, -1]).astype(states.dtype)

    def step(carry, inp):
        st, dec = inp
        return carry * dec[..., None, None] + st, carry

    _, prev = lax.scan(step, jnp.zeros_like(states[:, 0]),
                       (jnp.moveaxis(states, 1, 0), jnp.moveaxis(chunk_decay, 1, 0)))
    prev = jnp.moveaxis(prev, 0, 1)
    y_off = jnp.einsum('bctgn,bcghpn->bctghp', Cm, prev) * jnp.exp(a_cum)[..., None].astype(x.dtype)
    y = y_diag + y_off + x * d_skip.reshape(G, hg)[:, :, None]
    y = y.reshape(Bsz, S, SSD_INNER) * jax.nn.silu(z)
    y = rmsnorm(y.reshape(Bsz, S, G, SSD_INNER // G), norm_g.reshape(G, SSD_INNER // G))
    return y.reshape(Bsz, S, SSD_INNER)


def stick_breaking_attention(qkv):
    Bsz, S, _ = qkv.shape
    q, k, v = jnp.split(qkv, 3, axis=-1)
    q = q.reshape(Bsz, S, SB_HEADS, SB_HEAD_DIM).transpose(0, 2, 1, 3)
    k = k.reshape(Bsz, S, SB_HEADS, SB_HEAD_DIM).transpose(0, 2, 1, 3)
    v = v.reshape(Bsz, S, SB_HEADS, SB_HEAD_DIM).transpose(0, 2, 1, 3)
    scale = SB_HEAD_DIM ** -0.5
    outs = []
    for blk in range(S // SB_BLOCK):
        t0 = blk * SB_BLOCK
        t1 = t0 + SB_BLOCK
        zb = jnp.einsum('bhtd,bhsd->bhts', q[:, :, t0:t1], k[:, :, :t1]).astype(jnp.float32) * scale
        t_idx = t0 + jnp.arange(SB_BLOCK)
        s_idx = jnp.arange(t1)
        mask = s_idx[None, :] < t_idx[:, None]
        log_keep = jnp.where(mask, jax.nn.log_sigmoid(-zb), 0.0)
        log_between = lax.cumsum(log_keep, axis=3, reverse=True) - log_keep
        att = jnp.where(mask, jnp.exp(jax.nn.log_sigmoid(zb) + log_between), 0.0)
        outs.append(jnp.einsum('bhts,bhsd->bthd', att.astype(v.dtype), v[:, :, :t1]))
    return jnp.concatenate(outs, axis=1).reshape(Bsz, S, SB_WIDTH)


def cross_attention(h, mem_n, wq, wk, wv, wo):
    Bsz, S, _ = h.shape
    M = mem_n.shape[1]
    q = (h @ wq).reshape(Bsz, S, XA_HEADS, XA_HEAD_DIM)
    k = (mem_n @ wk).reshape(Bsz, M, XA_HEADS, XA_HEAD_DIM)
    v = (mem_n @ wv).reshape(Bsz, M, XA_HEADS, XA_HEAD_DIM)
    s = jnp.einsum('bshd,bmhd->bhsm', q, k).astype(jnp.float32) * (XA_HEAD_DIM ** -0.5)
    p = jax.nn.softmax(s, axis=-1).astype(v.dtype)
    o = jnp.einsum('bhsm,bmhd->bshd', p, v).reshape(Bsz, S, XA_WIDTH)
    return o @ wo


def swiglu(u, w_gate, w_up, w_down):
    return (jax.nn.silu(u @ w_gate) * (u @ w_up)) @ w_down


def moe_ffn(u, w_router, w_gate, w_up, w_down):
    logits = (u @ w_router).astype(jnp.float32)
    top_val, top_idx = lax.top_k(logits, TOP_K)
    top_w = jax.nn.softmax(top_val, axis=-1)
    combine = jnp.sum(jax.nn.one_hot(top_idx, N_EXPERTS, dtype=jnp.float32) * top_w[..., None],
                      axis=-2).astype(u.dtype)
    out = jnp.zeros_like(u)
    for e in range(N_EXPERTS):
        out = out + combine[..., e:e + 1] * swiglu(u, w_gate[e], w_up[e], w_down[e])
    return out


def setup_inputs(seed: int = 0) -> dict:
    key = jax.random.key(seed)
    ks = jax.random.split(key, 32)

    def dense(k, shape, fan_in):
        return jax.random.normal(k, shape, jnp.float32) * (fan_in ** -0.5)

    def gain(k, shape):
        return 1.0 + 0.05 * jax.random.normal(k, shape, jnp.float32)

    dt0 = jnp.exp(jax.random.uniform(ks[7], (DEPTH, SSD_HEADS), jnp.float32,
                                     math.log(1e-3), math.log(1e-1)))
    return {
        "x": jax.random.normal(ks[0], (BATCH, SEQ, D_MODEL), jnp.float32),
        "mem": jax.random.normal(ks[1], (BATCH, MEM_LEN, D_MODEL), jnp.float32),
        "norm_mix": gain(ks[2], (DEPTH, D_MODEL)),
        "w_in": dense(ks[3], (DEPTH, D_MODEL, IN_WIDTH), D_MODEL),
        "conv_w": dense(ks[4], (DEPTH, SSD_CONV, SSD_CONV_DIM), SSD_CONV),
        "conv_b": 0.02 * jax.random.normal(ks[5], (DEPTH, SSD_CONV_DIM), jnp.float32),
        "dt_bias": dt0 + jnp.log(-jnp.expm1(-dt0)),
        "a_log": jnp.log(jax.random.uniform(ks[8], (DEPTH, SSD_HEADS), jnp.float32, 1.0, 16.0)),
        "d_skip": gain(ks[9], (DEPTH, SSD_HEADS)),
        "ssd_norm": gain(ks[10], (DEPTH, SSD_INNER)),
        "w_ssd_o": dense(ks[11], (DEPTH, SSD_INNER, D_MODEL), SSD_INNER),
        "w_sb_o": dense(ks[12], (DEPTH, SB_WIDTH, D_MODEL), SB_WIDTH),
        "w_out": dense(ks[13], (DEPTH, D_MODEL, D_MODEL), D_MODEL),
        "norm_xa": gain(ks[14], (DEPTH, D_MODEL)),
        "norm_mem": gain(ks[15], (DEPTH, D_MODEL)),
        "xa_wq": dense(ks[16], (DEPTH, D_MODEL, XA_WIDTH), D_MODEL),
        "xa_wk": dense(ks[17], (DEPTH, D_MODEL, XA_WIDTH), D_MODEL),
        "xa_wv": dense(ks[18], (DEPTH, D_MODEL, XA_WIDTH), D_MODEL),
        "xa_wo": dense(ks[19], (DEPTH, XA_WIDTH, D_MODEL), XA_WIDTH),
        "norm_ffn": gain(ks[20], (DEPTH, D_MODEL)),
        "ffn_w_gate": dense(ks[21], (N_DENSE, D_MODEL, D_FF), D_MODEL),
        "ffn_w_up": dense(ks[22], (N_DENSE, D_MODEL, D_FF), D_MODEL),
        "ffn_w_down": dense(ks[23], (N_DENSE, D_FF, D_MODEL), D_FF),
        "moe_router": dense(ks[24], (N_MOE, D_MODEL, N_EXPERTS), D_MODEL),
        "moe_w_gate": dense(ks[25], (N_MOE, N_EXPERTS, D_MODEL, D_FF), D_MODEL),
        "moe_w_up": dense(ks[26], (N_MOE, N_EXPERTS, D_MODEL, D_FF), D_MODEL),
        "moe_w_down": dense(ks[27], (N_MOE, N_EXPERTS, D_FF, D_MODEL), D_FF),
        "final_norm": gain(ks[28], (D_MODEL,)),
    }


def reference(x, mem, norm_mix, w_in, conv_w, conv_b, dt_bias, a_log, d_skip, ssd_norm,
              w_ssd_o, w_sb_o, w_out, norm_xa, norm_mem, xa_wq, xa_wk, xa_wv, xa_wo,
              norm_ffn, ffn_w_gate, ffn_w_up, ffn_w_down, moe_router, moe_w_gate,
              moe_w_up, moe_w_down, final_norm):
    offsets = [SSD_INNER,
               SSD_INNER + SSD_CONV_DIM,
               SSD_INNER + SSD_CONV_DIM + SSD_HEADS,
               SSD_INNER + SSD_CONV_DIM + SSD_HEADS + 3 * SB_WIDTH,
               SSD_INNER + SSD_CONV_DIM + SSD_HEADS + 3 * SB_WIDTH + D_MODEL]
    h = x
    for i in range(DEPTH):
        u = rmsnorm(h, norm_mix[i])
        proj = u @ w_in[i]
        z, xbc, dt_raw, qkv, g_ssd, g_sb = jnp.split(proj, offsets, axis=-1)
        y_ssd = ssd_branch(z, xbc, dt_raw, conv_w[i], conv_b[i], dt_bias[i], a_log[i],
                           d_skip[i], ssd_norm[i])
        y_sb = stick_breaking_attention(qkv)
        merged = (jax.nn.sigmoid(g_ssd) * (y_ssd @ w_ssd_o[i])
                  + jax.nn.sigmoid(g_sb) * (y_sb @ w_sb_o[i]))
        h = h + merged @ w_out[i]
        h = h + cross_attention(rmsnorm(h, norm_xa[i]), rmsnorm(mem, norm_mem[i]),
                                xa_wq[i], xa_wk[i], xa_wv[i], xa_wo[i])
        u = rmsnorm(h, norm_ffn[i])
        j = i // 2
        if i % 2 == 0:
            h = h + swiglu(u, ffn_w_gate[j], ffn_w_up[j], ffn_w_down[j])
        else:
            h = h + moe_ffn(u, moe_router[j], moe_w_gate[j], moe_w_up[j], moe_w_down[j])
    return rmsnorm(h, final_norm)
```

```python
import functools

import jax
import jax.numpy as jnp
from jax import lax
from jax.experimental import pallas as pl
from jax.experimental.pallas import tpu as pltpu

F32 = jnp.float32
BF16 = jnp.bfloat16

D_MODEL = 1024
SSD_HEAD_DIM = 64
SSD_INNER = 1024
SSD_HEADS = 16
SSD_GROUPS = 2
SSD_STATE = 64
SSD_CONV = 4
SSD_CHUNK = 128
SSD_CONV_DIM = SSD_INNER + 2 * SSD_GROUPS * SSD_STATE
SB_HEADS = 8
SB_HEAD_DIM = 64
SB_WIDTH = 512
SB_BLOCK = 128
XA_HEADS = 4
XA_HEAD_DIM = 128
XA_WIDTH = 512
D_FF = 2816
N_EXPERTS = 8
TOP_K = 2
EPS = 1e-6

LANES = 128
VMEM_LIMIT = 56 * 1024 * 1024
ROW_TILE = 512
FF_CHUNK = 256
MOE_TILE = 512

SEG_Z = (0, 1024)
SEG_XBC = (1024, 1280)
SEG_DT = (2304, 128)
SEG_Q = (2432, 512)
SEG_K = (2944, 512)
SEG_V = (3456, 512)
SEG_GSSD = (3968, 1024)
SEG_GSB = (4992, 1024)
IN_PAD_WIDTH = 6016


def _cparams(sem):
    return pltpu.CompilerParams(dimension_semantics=sem, vmem_limit_bytes=VMEM_LIMIT)


def _rms(x, g):
    return x * lax.rsqrt(jnp.mean(x * x, axis=-1, keepdims=True) + EPS) * g


def _split2(x):
    hi = x.astype(BF16)
    lo = (x - hi.astype(F32)).astype(BF16)
    return hi, lo


def _dot(a, b):
    return jnp.dot(a, b, preferred_element_type=F32)


def _dot_nt(a, b):
    return lax.dot_general(a, b, (((1,), (1,)), ((), ())), preferred_element_type=F32)


def _dot_tn(a, b):
    return lax.dot_general(a, b, (((0,), (0,)), ((), ())), preferred_element_type=F32)


def _dot_split(x_f32, m_bf16):
    hi, lo = _split2(x_f32)
    return _dot(hi, m_bf16) + _dot(lo, m_bf16)


def _silu(x):
    return x / (1.0 + jnp.exp(-x))


def _sigmoid(x):
    return 1.0 / (1.0 + jnp.exp(-x))


def _norm_proj_kernel(x_ref, g_ref, w_ref, *out_refs, segs):
    x = x_ref[...]
    u = _rms(x, g_ref[...]).astype(BF16)
    for (start, width), o_ref in zip(segs, out_refs):
        for c0 in range(0, width, 512):
            cw = min(512, width - c0)
            r = _dot(u, w_ref[:, start + c0:start + c0 + cw])
            o_ref[:, c0:c0 + cw] = r.astype(o_ref.dtype)


def norm_proj(x, g, w, segs, dtypes, name):
    m, k = x.shape
    tm = min(ROW_TILE, m)
    n = w.shape[1]
    out_shape = [jax.ShapeDtypeStruct((m, wd), dt) for (_, wd), dt in zip(segs, dtypes)]
    out_specs = [pl.BlockSpec((tm, wd), lambda i: (i, 0)) for (_, wd) in segs]
    return pl.pallas_call(
        functools.partial(_norm_proj_kernel, segs=tuple(segs)),
        grid=(m // tm,),
        in_specs=[pl.BlockSpec((tm, k), lambda i: (i, 0)),
                  pl.BlockSpec((1, k), lambda i: (0, 0)),
                  pl.BlockSpec((k, n), lambda i: (0, 0))],
        out_specs=out_specs,
        out_shape=out_shape,
        compiler_params=_cparams(("parallel",)),
        name=name,
    )(x, g.reshape(1, k), w)


def _ssd_kernel(z_ref, xbc_ref, dt_ref, cw_ref, cb_ref, dtb_ref, aneg_ref, dsk_ref,
                ng_ref, e_ref, o_ref, xs_scr, state_scr, y_scr):
    L = SSD_CHUNK
    c = pl.program_id(1)

    @pl.when(c == 0)
    def _():
        xs_scr[0:8, :] = jnp.zeros((8, SSD_CONV_DIM), F32)
        state_scr[...] = jnp.zeros_like(state_scr)

    xs_scr[8:8 + L, :] = xbc_ref[0].astype(F32)
    acc = jnp.broadcast_to(cb_ref[...], (L, SSD_CONV_DIM))
    for k in range(SSD_CONV):
        off = 8 - (SSD_CONV - 1) + k
        acc = acc + xs_scr[off:off + L, :] * cw_ref[k:k + 1, :]
    xs_scr[0:8, :] = xs_scr[L:L + 8, :]
    xc = _silu(acc)
    xs = xc[:, :SSD_INNER]
    b128 = xc[:, SSD_INNER:SSD_INNER + LANES].astype(BF16)
    c128 = xc[:, SSD_INNER + LANES:SSD_INNER + 2 * LANES]

    row = lax.broadcasted_iota(jnp.int32, (L, L), 0)
    col = lax.broadcasted_iota(jnp.int32, (L, L), 1)
    causal = col <= row
    tril = jnp.where(causal, 1.0, 0.0).astype(BF16)

    dt_in = dt_ref[0] + dtb_ref[...]
    dt = jnp.maximum(dt_in, 0.0) + jnp.log1p(jnp.exp(-jnp.abs(dt_in)))
    a = dt * aneg_ref[...]
    a_hi = a.astype(BF16)
    a_mid = (a - a_hi.astype(F32))
    a_mid_b = a_mid.astype(BF16)
    a_lo = (a_mid - a_mid_b.astype(F32)).astype(BF16)
    acum = _dot(tril, a_hi) + _dot(tril, a_mid_b) + _dot(tril, a_lo)
    acum_t = acum.T
    alast = acum[L - 1:L, :]

    expand = e_ref[...]
    dt_e = _dot_split(dt, expand)
    ea_e = _dot_split(jnp.exp(acum), expand)
    dte_e = _dot_split(jnp.exp(alast - acum), expand)
    cd_e = _dot_split(jnp.broadcast_to(jnp.exp(alast), (8, LANES)), expand)[0:1, :]

    xdt = xs * dt_e
    xdt_b = xdt.astype(BF16)
    xw_b = (xdt * dte_e).astype(BF16)

    lane_half = lax.broadcasted_iota(jnp.int32, (L, LANES), 1) < SSD_STATE
    zero_b = jnp.zeros((L, LANES), BF16)
    c128_b = c128.astype(BF16)
    cb_g = [
        _dot_nt(jnp.where(lane_half, c128_b, zero_b), b128),
        _dot_nt(jnp.where(lane_half, zero_b, c128_b), b128),
    ]
    heads_per_group = SSD_HEADS // SSD_GROUPS
    for p in range(SSD_HEADS // 2):
        xp = xdt_b[:, p * LANES:(p + 1) * LANES]
        y_pair = None
        for s in range(2):
            h = 2 * p + s
            seg = acum[:, h:h + 1] - acum_t[h:h + 1, :]
            decay = jnp.exp(jnp.where(causal, seg, -jnp.inf))
            m_h = (cb_g[h // heads_per_group] * decay).astype(BF16)
            x_h = jnp.where(lane_half, xp, zero_b) if s == 0 else jnp.where(lane_half, zero_b, xp)
            t = _dot(m_h, x_h)
            y_pair = t if y_pair is None else y_pair + t
        y_scr[:, p * LANES:(p + 1) * LANES] = y_pair

    state = state_scr[...]
    y_off = _dot(c128_b, state.astype(BF16)) * ea_e
    s_new = _dot_tn(b128, xw_b)
    srow = lax.broadcasted_iota(jnp.int32, (LANES, SSD_INNER), 0) < SSD_STATE
    scol = lax.broadcasted_iota(jnp.int32, (LANES, SSD_INNER), 1) < SSD_INNER // SSD_GROUPS
    state_scr[...] = state * cd_e + jnp.where(srow == scol, s_new, 0.0)

    y = y_scr[...] + y_off + xs * dsk_ref[...]
    y = y * _silu(z_ref[0].astype(F32))
    gw = SSD_INNER // SSD_GROUPS
    for g in range(SSD_GROUPS):
        yg = y[:, g * gw:(g + 1) * gw]
        o_ref[0, :, g * gw:(g + 1) * gw] = _rms(yg, ng_ref[:, g * gw:(g + 1) * gw]).astype(o_ref.dtype)


def ssd_branch(z, xbc, dt, conv_w, conv_b, dt_bias, a_log, d_skip, norm_g):
    bsz, s, _ = z.shape
    nc = s // SSD_CHUNK
    pad = LANES - SSD_HEADS
    dtb = jnp.pad(dt_bias.astype(F32), (0, pad)).reshape(1, LANES)
    aneg = jnp.pad(-jnp.exp(a_log.astype(F32)), (0, pad)).reshape(1, LANES)
    dsk = jnp.repeat(d_skip.astype(F32), SSD_HEAD_DIM).reshape(1, SSD_INNER)
    expand = (jnp.arange(LANES)[:, None] == (jnp.arange(SSD_INNER)[None, :] // SSD_HEAD_DIM)).astype(BF16)
    const = lambda shape: pl.BlockSpec(shape, lambda b, c: (0,) * len(shape))
    return pl.pallas_call(
        _ssd_kernel,
        grid=(bsz, nc),
        in_specs=[pl.BlockSpec((1, SSD_CHUNK, SSD_INNER), lambda b, c: (b, c, 0)),
                  pl.BlockSpec((1, SSD_CHUNK, SSD_CONV_DIM), lambda b, c: (b, c, 0)),
                  pl.BlockSpec((1, SSD_CHUNK, LANES), lambda b, c: (b, c, 0)),
                  const((SSD_CONV, SSD_CONV_DIM)),
                  const((1, SSD_CONV_DIM)),
                  const((1, LANES)),
                  const((1, LANES)),
                  const((1, SSD_INNER)),
                  const((1, SSD_INNER)),
                  const((LANES, SSD_INNER))],
        out_specs=pl.BlockSpec((1, SSD_CHUNK, SSD_INNER), lambda b, c: (b, c, 0)),
        out_shape=jax.ShapeDtypeStruct((bsz, s, SSD_INNER), BF16),
        scratch_shapes=[pltpu.VMEM((SSD_CHUNK + 8, SSD_CONV_DIM), F32),
                        pltpu.VMEM((LANES, SSD_INNER), F32),
                        pltpu.VMEM((SSD_CHUNK, SSD_INNER), F32)],
        compiler_params=_cparams(("parallel", "arbitrary")),
        name="ssd_scan",
    )(z, xbc, dt, conv_w.astype(F32), conv_b.astype(F32).reshape(1, -1), dtb, aneg, dsk,
      norm_g.astype(F32).reshape(1, -1), expand)


def _sb_kernel(q_ref, k_ref, v_ref, o_ref):
    T = SB_BLOCK
    qi = pl.program_id(2)
    scale = SB_HEAD_DIM ** -0.5
    q = q_ref[0]
    lane_lo = lax.broadcasted_iota(jnp.int32, (T, LANES), 1) < SB_HEAD_DIM
    zero_b = jnp.zeros((T, LANES), BF16)
    qh = (jnp.where(lane_lo, q, zero_b), jnp.where(lane_lo, zero_b, q))
    row = lax.broadcasted_iota(jnp.int32, (T, T), 0)
    col = lax.broadcasted_iota(jnp.int32, (T, T), 1)
    strict = col < row
    r2 = lax.broadcasted_iota(jnp.int32, (T, 2 * T), 0)
    c2 = lax.broadcasted_iota(jnp.int32, (T, 2 * T), 1)
    suffix = jnp.where((c2 >= T) | (r2 > c2), 1.0, 0.0).astype(BF16)

    def block(j, carry, diag):
        start = pl.multiple_of(j * T, T)
        kb = k_ref[0, pl.ds(start, T), :]
        vb = v_ref[0, pl.ds(start, T), :]
        new = []
        for h in range(2):
            run, acc = carry[h]
            z = _dot_nt(qh[h], kb) * scale
            lk = -(jnp.maximum(z, 0.0) + jnp.log1p(jnp.exp(-jnp.abs(z))))
            lsig = z + lk
            if diag:
                lk = jnp.where(strict, lk, 0.0)
            both = _dot_split(lk, suffix)
            att = jnp.exp(lsig + both[:, :T] + run)
            if diag:
                att = jnp.where(strict, att, 0.0)
            acc = acc + _dot(att.astype(BF16), vb)
            new.append((run + both[:, T:], acc))
        return tuple(new)

    zeros = jnp.zeros((T, T), F32)
    carry = block(qi, ((zeros, zeros), (zeros, zeros)), True)
    carry = lax.fori_loop(1, qi + 1, lambda i, cr: block(qi - i, cr, False), carry)
    o_ref[0] = jnp.where(lane_lo, carry[0][1], carry[1][1]).astype(o_ref.dtype)


def stick_breaking(q, k, v):
    bsz, s, _ = q.shape
    nq = s // SB_BLOCK
    return pl.pallas_call(
        _sb_kernel,
        grid=(bsz, SB_WIDTH // LANES, nq),
        in_specs=[pl.BlockSpec((1, SB_BLOCK, LANES), lambda b, p, i: (b, i, p)),
                  pl.BlockSpec((1, s, LANES), lambda b, p, i: (b, 0, p)),
                  pl.BlockSpec((1, s, LANES), lambda b, p, i: (b, 0, p))],
        out_specs=pl.BlockSpec((1, SB_BLOCK, LANES), lambda b, p, i: (b, i, p)),
        out_shape=jax.ShapeDtypeStruct((bsz, s, SB_WIDTH), BF16),
        compiler_params=_cparams(("parallel", "parallel", "arbitrary")),
        name="stick_breaking",
    )(q, k, v)


def _merge_kernel(h_ref, ys_ref, yb_ref, gs_ref, gb_ref, ws_ref, wb_ref, wo_ref, o_ref):
    ms = _sigmoid(gs_ref[...].astype(F32)) * _dot(ys_ref[...], ws_ref[...])
    mb = _sigmoid(gb_ref[...].astype(F32)) * _dot(yb_ref[...], wb_ref[...])
    merged = (ms + mb).astype(BF16)
    o_ref[...] = h_ref[...] + _dot(merged, wo_ref[...])


def merge_mixers(h, y_ssd, y_sb, g_ssd, g_sb, w_ssd_o, w_sb_o, w_out):
    m, d = h.shape
    tm = min(ROW_TILE, m)
    rowspec = lambda w: pl.BlockSpec((tm, w), lambda i: (i, 0))
    wspec = lambda r, c: pl.BlockSpec((r, c), lambda i: (0, 0))
    return pl.pallas_call(
        _merge_kernel,
        grid=(m // tm,),
        in_specs=[rowspec(d), rowspec(SSD_INNER), rowspec(SB_WIDTH), rowspec(d), rowspec(d),
                  wspec(SSD_INNER, d), wspec(SB_WIDTH, d), wspec(d, d)],
        out_specs=rowspec(d),
        out_shape=jax.ShapeDtypeStruct((m, d), F32),
        compiler_params=_cparams(("parallel",)),
        name="merge_mixers",
    )(h, y_ssd, y_sb, g_ssd, g_sb, w_ssd_o, w_sb_o, w_out)


def _xattn_kernel(h_ref, g_ref, wq_ref, kv_ref, wo_ref, o_ref):
    h = h_ref[0]
    u = _rms(h, g_ref[...]).astype(BF16)
    q = _dot(u, wq_ref[...]).astype(BF16)
    scale = XA_HEAD_DIM ** -0.5
    outs = []
    for hd in range(XA_HEADS):
        lo = hd * XA_HEAD_DIM
        kh = kv_ref[0, :, lo:lo + XA_HEAD_DIM]
        vh = kv_ref[0, :, XA_WIDTH + lo:XA_WIDTH + lo + XA_HEAD_DIM]
        s = _dot_nt(q[:, lo:lo + XA_HEAD_DIM], kh) * scale
        e = jnp.exp(s - jnp.max(s, axis=-1, keepdims=True))
        oh = _dot(e.astype(BF16), vh) / jnp.sum(e, axis=-1, keepdims=True)
        outs.append(oh.astype(BF16))
    o = jnp.concatenate(outs, axis=-1)
    o_ref[0] = h + _dot(o, wo_ref[...])


def cross_attention(h, g, wq, kv, wo):
    bsz, s, d = h.shape
    tm = min(ROW_TILE, s)
    mem_len = kv.shape[1]
    return pl.pallas_call(
        _xattn_kernel,
        grid=(bsz, s // tm),
        in_specs=[pl.BlockSpec((1, tm, d), lambda b, i: (b, i, 0)),
                  pl.BlockSpec((1, d), lambda b, i: (0, 0)),
                  pl.BlockSpec((d, XA_WIDTH), lambda b, i: (0, 0)),
                  pl.BlockSpec((1, mem_len, 2 * XA_WIDTH), lambda b, i: (b, 0, 0)),
                  pl.BlockSpec((XA_WIDTH, d), lambda b, i: (0, 0))],
        out_specs=pl.BlockSpec((1, tm, d), lambda b, i: (b, i, 0)),
        out_shape=jax.ShapeDtypeStruct((bsz, s, d), F32),
        compiler_params=_cparams(("parallel", "parallel")),
        name="cross_attention",
    )(h, g.reshape(1, d), wq, kv, wo)


def _swiglu_tile(u, wg_ref, wu_ref, wd_ref):
    acc = None
    for c0 in range(0, D_FF, FF_CHUNK):
        gt = _dot(u, wg_ref[0, :, c0:c0 + FF_CHUNK])
        up = _dot(u, wu_ref[0, :, c0:c0 + FF_CHUNK])
        hid = (_silu(gt) * up).astype(BF16)
        t = _dot(hid, wd_ref[0, c0:c0 + FF_CHUNK, :])
        acc = t if acc is None else acc + t
    return acc


def _dense_ffn_kernel(h_ref, g_ref, wg_ref, wu_ref, wd_ref, o_ref):
    h = h_ref[...]
    u = _rms(h, g_ref[...]).astype(BF16)
    o_ref[...] = h + _swiglu_tile(u, wg_ref, wu_ref, wd_ref)


def dense_ffn(h, g, wg, wu, wd):
    m, d = h.shape
    tm = min(ROW_TILE, m)
    return pl.pallas_call(
        _dense_ffn_kernel,
        grid=(m // tm,),
        in_specs=[pl.BlockSpec((tm, d), lambda i: (i, 0)),
                  pl.BlockSpec((1, d), lambda i: (0, 0)),
                  pl.BlockSpec((1, d, D_FF), lambda i: (0, 0, 0)),
                  pl.BlockSpec((1, d, D_FF), lambda i: (0, 0, 0)),
                  pl.BlockSpec((1, D_FF, d), lambda i: (0, 0, 0))],
        out_specs=pl.BlockSpec((tm, d), lambda i: (i, 0)),
        out_shape=jax.ShapeDtypeStruct((m, d), F32),
        compiler_params=_cparams(("parallel",)),
        name="dense_ffn",
    )(h, g.reshape(1, d), wg[None], wu[None], wd[None])


def _expert_ffn_kernel(te_ref, nt_ref, x_ref, wg_ref, wu_ref, wd_ref, o_ref):
    @pl.when(pl.program_id(0) < nt_ref[0])
    def _():
        o_ref[...] = _swiglu_tile(x_ref[...], wg_ref, wu_ref, wd_ref).astype(o_ref.dtype)


def expert_ffn(x_sorted, tile_expert, n_tiles_used, wg, wu, wd, tm):
    r, d = x_sorted.shape
    grid_spec = pltpu.PrefetchScalarGridSpec(
        num_scalar_prefetch=2,
        grid=(r // tm,),
        in_specs=[pl.BlockSpec((tm, d), lambda i, te, nt: (i, 0)),
                  pl.BlockSpec((1, d, D_FF), lambda i, te, nt: (te[i], 0, 0)),
                  pl.BlockSpec((1, d, D_FF), lambda i, te, nt: (te[i], 0, 0)),
                  pl.BlockSpec((1, D_FF, d), lambda i, te, nt: (te[i], 0, 0))],
        out_specs=pl.BlockSpec((tm, d), lambda i, te, nt: (i, 0)),
    )
    return pl.pallas_call(
        _expert_ffn_kernel,
        grid_spec=grid_spec,
        out_shape=jax.ShapeDtypeStruct((r, d), BF16),
        compiler_params=_cparams(("arbitrary",)),
        name="expert_ffn",
    )(tile_expert, n_tiles_used, x_sorted, wg, wu, wd)


def _router_kernel(h_ref, g_ref, wr_hi_ref, wr_lo_ref, u_ref, route_ref):
    uf = _rms(h_ref[...], g_ref[...])
    u_hi = uf.astype(BF16)
    u_ref[...] = u_hi
    u_lo = (uf - u_hi.astype(F32)).astype(BF16)
    logits = _dot(u_hi, wr_hi_ref[...]) + (_dot(u_lo, wr_hi_ref[...]) + _dot(u_hi, wr_lo_ref[...]))
    lane = lax.broadcasted_iota(jnp.int32, logits.shape, 1)
    logits = jnp.where(lane < N_EXPERTS, logits, -jnp.inf)
    m1 = jnp.max(logits, axis=-1, keepdims=True)
    i1 = jnp.min(jnp.where(logits == m1, lane, LANES), axis=-1, keepdims=True)
    rest = jnp.where(lane == i1, -jnp.inf, logits)
    m2 = jnp.max(rest, axis=-1, keepdims=True)
    i2 = jnp.min(jnp.where(rest == m2, lane, LANES), axis=-1, keepdims=True)
    e2 = jnp.exp(m2 - m1)
    w1 = 1.0 / (1.0 + e2)
    w2 = e2 / (1.0 + e2)
    route = jnp.where(lane == 0, w1, jnp.where(lane == 1, w2, 0.0))
    route = jnp.where(lane == 2, i1.astype(F32), route)
    route = jnp.where(lane == 3, i2.astype(F32), route)
    route_ref[...] = route


def router(h, g, w_router):
    m, d = h.shape
    tm = min(ROW_TILE, m)
    wr = jnp.pad(w_router.astype(F32), ((0, 0), (0, LANES - N_EXPERTS)))
    wr_hi = wr.astype(BF16)
    wr_lo = (wr - wr_hi.astype(F32)).astype(BF16)
    return pl.pallas_call(
        _router_kernel,
        grid=(m // tm,),
        in_specs=[pl.BlockSpec((tm, d), lambda i: (i, 0)),
                  pl.BlockSpec((1, d), lambda i: (0, 0)),
                  pl.BlockSpec((d, LANES), lambda i: (0, 0)),
                  pl.BlockSpec((d, LANES), lambda i: (0, 0))],
        out_specs=[pl.BlockSpec((tm, d), lambda i: (i, 0)),
                   pl.BlockSpec((tm, LANES), lambda i: (i, 0))],
        out_shape=[jax.ShapeDtypeStruct((m, d), BF16),
                   jax.ShapeDtypeStruct((m, LANES), F32)],
        compiler_params=_cparams(("parallel",)),
        name="router",
    )(h, g.reshape(1, d), wr_hi, wr_lo)


def _combine_kernel(h_ref, y0_ref, y1_ref, route_ref, g_ref, o_ref, *, final_norm):
    route = route_ref[...]
    w0 = route[:, 0:1]
    w1 = route[:, 1:2]
    out = h_ref[...] + (w0 * y0_ref[...].astype(F32) + w1 * y1_ref[...].astype(F32))
    if final_norm:
        out = _rms(out, g_ref[...])
    o_ref[...] = out


def combine(h, y0, y1, route, g, final_norm):
    m, d = h.shape
    tm = min(ROW_TILE, m)
    rowspec = lambda w: pl.BlockSpec((tm, w), lambda i: (i, 0))
    return pl.pallas_call(
        functools.partial(_combine_kernel, final_norm=final_norm),
        grid=(m // tm,),
        in_specs=[rowspec(d), rowspec(d), rowspec(d), rowspec(LANES),
                  pl.BlockSpec((1, d), lambda i: (0, 0))],
        out_specs=rowspec(d),
        out_shape=jax.ShapeDtypeStruct((m, d), F32),
        compiler_params=_cparams(("parallel",)),
        name="moe_combine",
    )(h, y0, y1, route, g.reshape(1, d))


def _final_norm_kernel(h_ref, g_ref, o_ref):
    o_ref[...] = _rms(h_ref[...], g_ref[...])


def final_norm_rows(h, g):
    m, d = h.shape
    tm = min(ROW_TILE, m)
    return pl.pallas_call(
        _final_norm_kernel,
        grid=(m // tm,),
        in_specs=[pl.BlockSpec((tm, d), lambda i: (i, 0)), pl.BlockSpec((1, d), lambda i: (0, 0))],
        out_specs=pl.BlockSpec((tm, d), lambda i: (i, 0)),
        out_shape=jax.ShapeDtypeStruct((m, d), F32),
        compiler_params=_cparams(("parallel",)),
        name="final_norm",
    )(h, g.reshape(1, d))


def moe_ffn(h, g, w_router, wg, wu, wd, g_final, apply_final_norm):
    m, d = h.shape
    tm = min(MOE_TILE, m)
    u, route = router(h, g, w_router)
    experts = route[:, 2:4].astype(jnp.int32).reshape(-1)
    onehot = (experts[:, None] == jnp.arange(N_EXPERTS)[None, :]).astype(jnp.int32)
    ranks = jnp.cumsum(onehot, axis=0) - onehot
    rank = jnp.sum(ranks * onehot, axis=1)
    counts = jnp.sum(onehot, axis=0)
    tiles_per = (counts + tm - 1) // tm
    tile_end = jnp.cumsum(tiles_per)
    tile_start = tile_end - tiles_per
    pos = tile_start[experts] * tm + rank
    n_rows = TOP_K * m + N_EXPERTS * tm
    n_tiles = n_rows // tm
    row_token = jnp.zeros((n_rows,), jnp.int32).at[pos].set(jnp.arange(TOP_K * m, dtype=jnp.int32) // TOP_K)
    tile_expert = jnp.minimum(
        jnp.sum((jnp.arange(n_tiles)[:, None] >= tile_end[None, :]).astype(jnp.int32), axis=1),
        N_EXPERTS - 1).astype(jnp.int32)
    n_used = tile_end[-1:].astype(jnp.int32)
    x_sorted = jnp.take(u, row_token, axis=0)
    y = expert_ffn(x_sorted, tile_expert, n_used, wg, wu, wd, tm)
    pos2 = pos.reshape(m, TOP_K)
    y0 = jnp.take(y, pos2[:, 0], axis=0)
    y1 = jnp.take(y, pos2[:, 1], axis=0)
    return combine(h, y0, y1, route, g_final, apply_final_norm)


def _in_proj_weight(w):
    o_xbc = SSD_INNER
    o_dt = o_xbc + SSD_CONV_DIM
    o_qkv = o_dt + SSD_HEADS
    o_g = o_qkv + 3 * SB_WIDTH
    dt_cols = jnp.pad(w[:, o_dt:o_qkv], ((0, 0), (0, LANES - SSD_HEADS)))
    return jnp.concatenate([w[:, :o_dt], dt_cols, w[:, o_qkv:]], axis=1).astype(BF16)


def kernel(x, mem, norm_mix, w_in, conv_w, conv_b, dt_bias, a_log, d_skip, ssd_norm, w_ssd_o, w_sb_o, w_out, norm_xa, norm_mem, xa_wq, xa_wk, xa_wv, xa_wo, norm_ffn, ffn_w_gate, ffn_w_up, ffn_w_down, moe_router, moe_w_gate, moe_w_up, moe_w_down, final_norm):
    bsz, s, d = x.shape
    m = bsz * s
    depth = w_in.shape[0]
    mem2 = mem.reshape(-1, d)
    h = x.reshape(m, d)
    segs = (SEG_Z, SEG_XBC, SEG_DT, SEG_Q, SEG_K, SEG_V, SEG_GSSD, SEG_GSB)
    seg_dtypes = (BF16, BF16, F32, BF16, BF16, BF16, BF16, BF16)
    for i in range(depth):
        z, xbc, dt, q, k, v, g_ssd, g_sb = norm_proj(
            h, norm_mix[i], _in_proj_weight(w_in[i]), segs, seg_dtypes, "in_proj")
        sh = lambda a: a.reshape(bsz, s, a.shape[-1])
        y_ssd = ssd_branch(sh(z), sh(xbc), sh(dt), conv_w[i], conv_b[i], dt_bias[i], a_log[i],
                           d_skip[i], ssd_norm[i])
        y_sb = stick_breaking(sh(q), sh(k), sh(v))
        h = merge_mixers(h, y_ssd.reshape(m, -1), y_sb.reshape(m, -1), g_ssd, g_sb,
                         w_ssd_o[i].astype(BF16), w_sb_o[i].astype(BF16), w_out[i].astype(BF16))
        w_kv = jnp.concatenate([xa_wk[i], xa_wv[i]], axis=1).astype(BF16)
        (kv,) = norm_proj(mem2, norm_mem[i], w_kv, ((0, 2 * XA_WIDTH),), (BF16,), "mem_kv")
        h = cross_attention(h.reshape(bsz, s, d), norm_xa[i], xa_wq[i].astype(BF16),
                            kv.reshape(bsz, -1, 2 * XA_WIDTH), xa_wo[i].astype(BF16)).reshape(m, d)
        j = i // 2
        last = i == depth - 1
        if i % 2 == 0:
            h = dense_ffn(h, norm_ffn[i], ffn_w_gate[j].astype(BF16), ffn_w_up[j].astype(BF16),
                          ffn_w_down[j].astype(BF16))
            if last:
                h = final_norm_rows(h, final_norm)
        else:
            h = moe_ffn(h, norm_ffn[i], moe_router[j], moe_w_gate[j].astype(BF16),
                        moe_w_up[j].astype(BF16), moe_w_down[j].astype(BF16), final_norm, last)
    return h.reshape(bsz, s, d)
```

```python
import functools

import jax
import jax.numpy as jnp
from jax import lax
from jax.experimental import pallas as pl
from jax.experimental.pallas import tpu as pltpu

F32 = jnp.float32
BF16 = jnp.bfloat16

D_MODEL = 1024
SSD_HEAD_DIM = 64
SSD_INNER = 1024
SSD_HEADS = 16
SSD_GROUPS = 2
SSD_STATE = 64
SSD_CONV = 4
SSD_CHUNK = 128
SSD_CONV_DIM = SSD_INNER + 2 * SSD_GROUPS * SSD_STATE
SB_HEADS = 8
SB_HEAD_DIM = 64
SB_WIDTH = 512
SB_BLOCK = 128
XA_HEADS = 4
XA_HEAD_DIM = 128
XA_WIDTH = 512
D_FF = 2816
N_EXPERTS = 8
TOP_K = 2
EPS = 1e-6

LANES = 128
VMEM_LIMIT = 56 * 1024 * 1024
ROW_TILE = 512
FF_CHUNK = 256
MOE_TILE = 512

SEG_Z = (0, 1024)
SEG_XBC = (1024, 1280)
SEG_DT = (2304, 128)
SEG_Q = (2432, 512)
SEG_K = (2944, 512)
SEG_V = (3456, 512)
SEG_GSSD = (3968, 1024)
SEG_GSB = (4992, 1024)
IN_PAD_WIDTH = 6016


def _cparams(sem):
    return pltpu.CompilerParams(dimension_semantics=sem, vmem_limit_bytes=VMEM_LIMIT)


def _rms(x, g):
    return x * lax.rsqrt(jnp.mean(x * x, axis=-1, keepdims=True) + EPS) * g


def _split2(x):
    hi = x.astype(BF16)
    lo = (x - hi.astype(F32)).astype(BF16)
    return hi, lo


def _dot(a, b):
    return jnp.dot(a, b, preferred_element_type=F32)


def _dot_nt(a, b):
    return lax.dot_general(a, b, (((1,), (1,)), ((), ())), preferred_element_type=F32)


def _dot_tn(a, b):
    return lax.dot_general(a, b, (((0,), (0,)), ((), ())), preferred_element_type=F32)


def _dot_split(x_f32, m_bf16):
    hi, lo = _split2(x_f32)
    return _dot(hi, m_bf16) + _dot(lo, m_bf16)


def _silu(x):
    return x / (1.0 + jnp.exp(-x))


def _sigmoid(x):
    return 1.0 / (1.0 + jnp.exp(-x))


def _norm_proj_kernel(x_ref, g_ref, w_ref, *out_refs, segs):
    x = x_ref[...]
    u = _rms(x, g_ref[...]).astype(BF16)
    for (start, width), o_ref in zip(segs, out_refs):
        for c0 in range(0, width, 512):
            cw = min(512, width - c0)
            r = _dot(u, w_ref[:, start + c0:start + c0 + cw])
            o_ref[:, c0:c0 + cw] = r.astype(o_ref.dtype)


def norm_proj(x, g, w, segs, dtypes, name):
    m, k = x.shape
    tm = min(ROW_TILE, m)
    n = w.shape[1]
    out_shape = [jax.ShapeDtypeStruct((m, wd), dt) for (_, wd), dt in zip(segs, dtypes)]
    out_specs = [pl.BlockSpec((tm, wd), lambda i: (i, 0)) for (_, wd) in segs]
    return pl.pallas_call(
        functools.partial(_norm_proj_kernel, segs=tuple(segs)),
        grid=(m // tm,),
        in_specs=[pl.BlockSpec((tm, k), lambda i: (i, 0)),
                  pl.BlockSpec((1, k), lambda i: (0, 0)),
                  pl.BlockSpec((k, n), lambda i: (0, 0))],
        out_specs=out_specs,
        out_shape=out_shape,
        compiler_params=_cparams(("parallel",)),
        name=name,
    )(x, g.reshape(1, k), w)


def _ssd_kernel(z_ref, xbc_ref, dt_ref, cw_ref, cb_ref, dtb_ref, aneg_ref, dsk_ref,
                ng_ref, e_ref, o_ref, xs_scr, state_scr, y_scr):
    L = SSD_CHUNK
    c = pl.program_id(1)

    @pl.when(c == 0)
    def _():
        xs_scr[0:8, :] = jnp.zeros((8, SSD_CONV_DIM), F32)
        state_scr[...] = jnp.zeros_like(state_scr)

    xs_scr[8:8 + L, :] = xbc_ref[0].astype(F32)
    acc = jnp.broadcast_to(cb_ref[...], (L, SSD_CONV_DIM))
    for k in range(SSD_CONV):
        off = 8 - (SSD_CONV - 1) + k
        acc = acc + xs_scr[off:off + L, :] * cw_ref[k:k + 1, :]
    xs_scr[0:8, :] = xs_scr[L:L + 8, :]
    xc = _silu(acc)
    xs = xc[:, :SSD_INNER]
    b128 = xc[:, SSD_INNER:SSD_INNER + LANES].astype(BF16)
    c128 = xc[:, SSD_INNER + LANES:SSD_INNER + 2 * LANES]

    row = lax.broadcasted_iota(jnp.int32, (L, L), 0)
    col = lax.broadcasted_iota(jnp.int32, (L, L), 1)
    causal = col <= row
    tril = jnp.where(causal, 1.0, 0.0).astype(BF16)

    dt_in = dt_ref[0] + dtb_ref[...]
    dt = jnp.maximum(dt_in, 0.0) + jnp.log1p(jnp.exp(-jnp.abs(dt_in)))
    a = dt * aneg_ref[...]
    a_hi = a.astype(BF16)
    a_mid = (a - a_hi.astype(F32))
    a_mid_b = a_mid.astype(BF16)
    a_lo = (a_mid - a_mid_b.astype(F32)).astype(BF16)
    acum = _dot(tril, a_hi) + _dot(tril, a_mid_b) + _dot(tril, a_lo)
    acum_t = acum.T
    alast = acum[L - 1:L, :]

    expand = e_ref[...]
    dt_e = _dot_split(dt, expand)
    ea_e = _dot_split(jnp.exp(acum), expand)
    dte_e = _dot_split(jnp.exp(alast - acum), expand)
    cd_e = _dot_split(jnp.broadcast_to(jnp.exp(alast), (8, LANES)), expand)[0:1, :]

    xdt = xs * dt_e
    xdt_b = xdt.astype(BF16)
    xw_b = (xdt * dte_e).astype(BF16)

    lane_half = lax.broadcasted_iota(jnp.int32, (L, LANES), 1) < SSD_STATE
    zero_b = jnp.zeros((L, LANES), BF16)
    c128_b = c128.astype(BF16)
    cb_g = [
        _dot_nt(jnp.where(lane_half, c128_b, zero_b), b128),
        _dot_nt(jnp.where(lane_half, zero_b, c128_b), b128),
    ]
    heads_per_group = SSD_HEADS // SSD_GROUPS
    for p in range(SSD_HEADS // 2):
        xp = xdt_b[:, p * LANES:(p + 1) * LANES]
        y_pair = None
        for s in range(2):
            h = 2 * p + s
            seg = acum[:, h:h + 1] - acum_t[h:h + 1, :]
            decay = jnp.exp(jnp.where(causal, seg, -jnp.inf))
            m_h = (cb_g[h // heads_per_group] * decay).astype(BF16)
            x_h = jnp.where(lane_half, xp, zero_b) if s == 0 else jnp.where(lane_half, zero_b, xp)
            t = _dot(m_h, x_h)
            y_pair = t if y_pair is None else y_pair + t
        y_scr[:, p * LANES:(p + 1) * LANES] = y_pair

    state = state_scr[...]
    y_off = _dot(c128_b, state.astype(BF16)) * ea_e
    s_new = _dot_tn(b128, xw_b)
    srow = lax.broadcasted_iota(jnp.int32, (LANES, SSD_INNER), 0) < SSD_STATE
    scol = lax.broadcasted_iota(jnp.int32, (LANES, SSD_INNER), 1) < SSD_INNER // SSD_GROUPS
    state_scr[...] = state * cd_e + jnp.where(srow == scol, s_new, 0.0)

    y = y_scr[...] + y_off + xs * dsk_ref[...]
    y = y * _silu(z_ref[0].astype(F32))
    gw = SSD_INNER // SSD_GROUPS
    for g in range(SSD_GROUPS):
        yg = y[:, g * gw:(g + 1) * gw]
        o_ref[0, :, g * gw:(g + 1) * gw] = _rms(yg, ng_ref[:, g * gw:(g + 1) * gw]).astype(o_ref.dtype)


def ssd_branch(z, xbc, dt, conv_w, conv_b, dt_bias, a_log, d_skip, norm_g):
    bsz, s, _ = z.shape
    nc = s // SSD_CHUNK
    pad = LANES - SSD_HEADS
    dtb = jnp.pad(dt_bias.astype(F32), (0, pad)).reshape(1, LANES)
    aneg = jnp.pad(-jnp.exp(a_log.astype(F32)), (0, pad)).reshape(1, LANES)
    dsk = jnp.repeat(d_skip.astype(F32), SSD_HEAD_DIM).reshape(1, SSD_INNER)
    expand = (jnp.arange(LANES)[:, None] == (jnp.arange(SSD_INNER)[None, :] // SSD_HEAD_DIM)).astype(BF16)
    const = lambda shape: pl.BlockSpec(shape, lambda b, c: (0,) * len(shape))
    return pl.pallas_call(
        _ssd_kernel,
        grid=(bsz, nc),
        in_specs=[pl.BlockSpec((1, SSD_CHUNK, SSD_INNER), lambda b, c: (b, c, 0)),
                  pl.BlockSpec((1, SSD_CHUNK, SSD_CONV_DIM), lambda b, c: (b, c, 0)),
                  pl.BlockSpec((1, SSD_CHUNK, LANES), lambda b, c: (b, c, 0)),
                  const((SSD_CONV, SSD_CONV_DIM)),
                  const((1, SSD_CONV_DIM)),
                  const((1, LANES)),
                  const((1, LANES)),
                  const((1, SSD_INNER)),
                  const((1, SSD_INNER)),
                  const((LANES, SSD_INNER))],
        out_specs=pl.BlockSpec((1, SSD_CHUNK, SSD_INNER), lambda b, c: (b, c, 0)),
        out_shape=jax.ShapeDtypeStruct((bsz, s, SSD_INNER), BF16),
        scratch_shapes=[pltpu.VMEM((SSD_CHUNK + 8, SSD_CONV_DIM), F32),
                        pltpu.VMEM((LANES, SSD_INNER), F32),
                        pltpu.VMEM((SSD_CHUNK, SSD_INNER), F32)],
        compiler_params=_cparams(("parallel", "arbitrary")),
        name="ssd_scan",
    )(z, xbc, dt, conv_w.astype(F32), conv_b.astype(F32).reshape(1, -1), dtb, aneg, dsk,
      norm_g.astype(F32).reshape(1, -1), expand)


SB_PAIRS = SB_WIDTH // LANES
EXP_UNDERFLOW = -104.0


def _sb_kernel(q_ref, k_ref, v_ref, o_ref, run_scr, acc_scr):
    T = SB_BLOCK
    qi = pl.program_id(1)
    lane_lo = lax.broadcasted_iota(jnp.int32, (T, LANES), 1) < SB_HEAD_DIM
    zero_b = jnp.zeros((T, LANES), BF16)
    r2 = lax.broadcasted_iota(jnp.int32, (2 * T, 2 * T), 0)
    c2 = lax.broadcasted_iota(jnp.int32, (2 * T, 2 * T), 1)
    nsuffix = jnp.where(((r2 < T) == (c2 < T)) & (r2 > c2), -1.0, 0.0).astype(BF16)
    rw = lax.broadcasted_iota(jnp.int32, (T, 2 * T), 0)
    cw = lax.broadcasted_iota(jnp.int32, (T, 2 * T), 1)
    strict_w = (cw & (T - 1)) < rw
    strict = strict_w[:, :T]
    q_all = q_ref[0] * jnp.asarray(SB_HEAD_DIM ** -0.5, BF16)

    run_scr[...] = jnp.zeros_like(run_scr)
    acc_scr[...] = jnp.zeros_like(acc_scr)

    def block(j, diag):
        start = pl.multiple_of(j * T, T)
        kb_all = k_ref[0, pl.ds(start, T), :]
        vb_all = v_ref[0, pl.ds(start, T), :]
        runmax = None
        pairs = range(SB_PAIRS)
        zs = []
        for p in pairs:
            kb = kb_all[:, p * LANES:(p + 1) * LANES]
            kcat = jnp.concatenate([jnp.where(lane_lo, kb, zero_b), jnp.where(lane_lo, zero_b, kb)], axis=0)
            zs.append(_dot_nt(q_all[:, p * LANES:(p + 1) * LANES], kcat))
        sps, lsigs = [], []
        for p in pairs:
            z = zs[p]
            sp = jnp.maximum(z, 0.0) + jnp.log(1.0 + jnp.exp(-jnp.abs(z)))
            lsigs.append(z - sp)
            sps.append(jnp.where(strict_w, sp, 0.0) if diag else sp)
        lbs = [_dot_split(sps[p], nsuffix) for p in pairs]
        for p in pairs:
            args = lsigs[p] + lbs[p]
            atts = []
            for h in range(2):
                run = run_scr[2 * p + h]
                att = jnp.exp(args[:, h * T:(h + 1) * T] + run)
                if diag:
                    att = jnp.where(strict, att, 0.0)
                atts.append(att.astype(BF16))
                run_new = run[:, 0:1] - jnp.sum(sps[p][:, h * T:(h + 1) * T], axis=-1, keepdims=True)
                run_scr[2 * p + h] = jnp.broadcast_to(run_new, (T, LANES))
                runmax = run_new if runmax is None else jnp.maximum(runmax, run_new)
            vb = vb_all[:, p * LANES:(p + 1) * LANES]
            vcat = jnp.concatenate([jnp.where(lane_lo, vb, zero_b), jnp.where(lane_lo, zero_b, vb)], axis=0)
            acc_scr[p] += _dot(jnp.concatenate(atts, axis=1), vcat)
        return (jnp.max(runmax) > EXP_UNDERFLOW).astype(jnp.int32)

    go = block(qi, True)

    def body(carry):
        i, _ = carry
        return i + 1, block(qi - i, False)

    lax.while_loop(lambda c: (c[0] <= qi) & (c[1] > 0), body, (jnp.int32(1), go))
    for p in range(SB_PAIRS):
        o_ref[0, :, p * LANES:(p + 1) * LANES] = acc_scr[p].astype(o_ref.dtype)


def stick_breaking(q, k, v):
    bsz, s, _ = q.shape
    nq = s // SB_BLOCK
    return pl.pallas_call(
        _sb_kernel,
        grid=(bsz, nq),
        in_specs=[pl.BlockSpec((1, SB_BLOCK, SB_WIDTH), lambda b, i: (b, i, 0)),
                  pl.BlockSpec((1, s, SB_WIDTH), lambda b, i: (b, 0, 0)),
                  pl.BlockSpec((1, s, SB_WIDTH), lambda b, i: (b, 0, 0))],
        out_specs=pl.BlockSpec((1, SB_BLOCK, SB_WIDTH), lambda b, i: (b, i, 0)),
        out_shape=jax.ShapeDtypeStruct((bsz, s, SB_WIDTH), BF16),
        scratch_shapes=[pltpu.VMEM((SB_HEADS, SB_BLOCK, LANES), F32),
                        pltpu.VMEM((SB_PAIRS, SB_BLOCK, LANES), F32)],
        compiler_params=_cparams(("parallel", "arbitrary")),
        name="stick_breaking",
    )(q, k, v)


def _merge_kernel(h_ref, ys_ref, yb_ref, gs_ref, gb_ref, ws_ref, wb_ref, wo_ref, o_ref):
    ms = _sigmoid(gs_ref[...].astype(F32)) * _dot(ys_ref[...], ws_ref[...])
    mb = _sigmoid(gb_ref[...].astype(F32)) * _dot(yb_ref[...], wb_ref[...])
    merged = (ms + mb).astype(BF16)
    o_ref[...] = h_ref[...] + _dot(merged, wo_ref[...])


def merge_mixers(h, y_ssd, y_sb, g_ssd, g_sb, w_ssd_o, w_sb_o, w_out):
    m, d = h.shape
    tm = min(ROW_TILE, m)
    rowspec = lambda w: pl.BlockSpec((tm, w), lambda i: (i, 0))
    wspec = lambda r, c: pl.BlockSpec((r, c), lambda i: (0, 0))
    return pl.pallas_call(
        _merge_kernel,
        grid=(m // tm,),
        in_specs=[rowspec(d), rowspec(SSD_INNER), rowspec(SB_WIDTH), rowspec(d), rowspec(d),
                  wspec(SSD_INNER, d), wspec(SB_WIDTH, d), wspec(d, d)],
        out_specs=rowspec(d),
        out_shape=jax.ShapeDtypeStruct((m, d), F32),
        compiler_params=_cparams(("parallel",)),
        name="merge_mixers",
    )(h, y_ssd, y_sb, g_ssd, g_sb, w_ssd_o, w_sb_o, w_out)


def _xattn_kernel(h_ref, g_ref, wq_ref, kv_ref, wo_ref, o_ref):
    h = h_ref[0]
    u = _rms(h, g_ref[...]).astype(BF16)
    q = _dot(u, wq_ref[...]).astype(BF16)
    scale = XA_HEAD_DIM ** -0.5
    outs = []
    for hd in range(XA_HEADS):
        lo = hd * XA_HEAD_DIM
        kh = kv_ref[0, :, lo:lo + XA_HEAD_DIM]
        vh = kv_ref[0, :, XA_WIDTH + lo:XA_WIDTH + lo + XA_HEAD_DIM]
        s = _dot_nt(q[:, lo:lo + XA_HEAD_DIM], kh) * scale
        e = jnp.exp(s - jnp.max(s, axis=-1, keepdims=True))
        oh = _dot(e.astype(BF16), vh) / jnp.sum(e, axis=-1, keepdims=True)
        outs.append(oh.astype(BF16))
    o = jnp.concatenate(outs, axis=-1)
    o_ref[0] = h + _dot(o, wo_ref[...])


def cross_attention(h, g, wq, kv, wo):
    bsz, s, d = h.shape
    tm = min(ROW_TILE, s)
    mem_len = kv.shape[1]
    return pl.pallas_call(
        _xattn_kernel,
        grid=(bsz, s // tm),
        in_specs=[pl.BlockSpec((1, tm, d), lambda b, i: (b, i, 0)),
                  pl.BlockSpec((1, d), lambda b, i: (0, 0)),
                  pl.BlockSpec((d, XA_WIDTH), lambda b, i: (0, 0)),
                  pl.BlockSpec((1, mem_len, 2 * XA_WIDTH), lambda b, i: (b, 0, 0)),
                  pl.BlockSpec((XA_WIDTH, d), lambda b, i: (0, 0))],
        out_specs=pl.BlockSpec((1, tm, d), lambda b, i: (b, i, 0)),
        out_shape=jax.ShapeDtypeStruct((bsz, s, d), F32),
        compiler_params=_cparams(("parallel", "parallel")),
        name="cross_attention",
    )(h, g.reshape(1, d), wq, kv, wo)


def _swiglu_tile(u, wg_ref, wu_ref, wd_ref):
    acc = None
    for c0 in range(0, D_FF, FF_CHUNK):
        gt = _dot(u, wg_ref[0, :, c0:c0 + FF_CHUNK])
        up = _dot(u, wu_ref[0, :, c0:c0 + FF_CHUNK])
        hid = (_silu(gt) * up).astype(BF16)
        t = _dot(hid, wd_ref[0, c0:c0 + FF_CHUNK, :])
        acc = t if acc is None else acc + t
    return acc


def _dense_ffn_kernel(h_ref, g_ref, wg_ref, wu_ref, wd_ref, o_ref):
    h = h_ref[...]
    u = _rms(h, g_ref[...]).astype(BF16)
    o_ref[...] = h + _swiglu_tile(u, wg_ref, wu_ref, wd_ref)


def dense_ffn(h, g, wg, wu, wd):
    m, d = h.shape
    tm = min(ROW_TILE, m)
    return pl.pallas_call(
        _dense_ffn_kernel,
        grid=(m // tm,),
        in_specs=[pl.BlockSpec((tm, d), lambda i: (i, 0)),
                  pl.BlockSpec((1, d), lambda i: (0, 0)),
                  pl.BlockSpec((1, d, D_FF), lambda i: (0, 0, 0)),
                  pl.BlockSpec((1, d, D_FF), lambda i: (0, 0, 0)),
                  pl.BlockSpec((1, D_FF, d), lambda i: (0, 0, 0))],
        out_specs=pl.BlockSpec((tm, d), lambda i: (i, 0)),
        out_shape=jax.ShapeDtypeStruct((m, d), F32),
        compiler_params=_cparams(("parallel",)),
        name="dense_ffn",
    )(h, g.reshape(1, d), wg[None], wu[None], wd[None])


def _expert_ffn_kernel(te_ref, nt_ref, x_ref, wg_ref, wu_ref, wd_ref, o_ref):
    @pl.when(pl.program_id(0) < nt_ref[0])
    def _():
        o_ref[...] = _swiglu_tile(x_ref[...], wg_ref, wu_ref, wd_ref).astype(o_ref.dtype)


def expert_ffn(x_sorted, tile_expert, n_tiles_used, wg, wu, wd, tm):
    r, d = x_sorted.shape
    grid_spec = pltpu.PrefetchScalarGridSpec(
        num_scalar_prefetch=2,
        grid=(r // tm,),
        in_specs=[pl.BlockSpec((tm, d), lambda i, te, nt: (i, 0)),
                  pl.BlockSpec((1, d, D_FF), lambda i, te, nt: (te[i], 0, 0)),
                  pl.BlockSpec((1, d, D_FF), lambda i, te, nt: (te[i], 0, 0)),
                  pl.BlockSpec((1, D_FF, d), lambda i, te, nt: (te[i], 0, 0))],
        out_specs=pl.BlockSpec((tm, d), lambda i, te, nt: (i, 0)),
    )
    return pl.pallas_call(
        _expert_ffn_kernel,
        grid_spec=grid_spec,
        out_shape=jax.ShapeDtypeStruct((r, d), BF16),
        compiler_params=_cparams(("arbitrary",)),
        name="expert_ffn",
    )(tile_expert, n_tiles_used, x_sorted, wg, wu, wd)


def _router_kernel(h_ref, g_ref, wr_hi_ref, wr_lo_ref, u_ref, route_ref):
    uf = _rms(h_ref[...], g_ref[...])
    u_hi = uf.astype(BF16)
    u_ref[...] = u_hi
    u_lo = (uf - u_hi.astype(F32)).astype(BF16)
    logits = _dot(u_hi, wr_hi_ref[...]) + (_dot(u_lo, wr_hi_ref[...]) + _dot(u_hi, wr_lo_ref[...]))
    lane = lax.broadcasted_iota(jnp.int32, logits.shape, 1)
    logits = jnp.where(lane < N_EXPERTS, logits, -jnp.inf)
    m1 = jnp.max(logits, axis=-1, keepdims=True)
    i1 = jnp.min(jnp.where(logits == m1, lane, LANES), axis=-1, keepdims=True)
    rest = jnp.where(lane == i1, -jnp.inf, logits)
    m2 = jnp.max(rest, axis=-1, keepdims=True)
    i2 = jnp.min(jnp.where(rest == m2, lane, LANES), axis=-1, keepdims=True)
    e2 = jnp.exp(m2 - m1)
    w1 = 1.0 / (1.0 + e2)
    w2 = e2 / (1.0 + e2)
    route = jnp.where(lane == 0, w1, jnp.where(lane == 1, w2, 0.0))
    route = jnp.where(lane == 2, i1.astype(F32), route)
    route = jnp.where(lane == 3, i2.astype(F32), route)
    route_ref[...] = route


def router(h, g, w_router):
    m, d = h.shape
    tm = min(ROW_TILE, m)
    wr = jnp.pad(w_router.astype(F32), ((0, 0), (0, LANES - N_EXPERTS)))
    wr_hi = wr.astype(BF16)
    wr_lo = (wr - wr_hi.astype(F32)).astype(BF16)
    return pl.pallas_call(
        _router_kernel,
        grid=(m // tm,),
        in_specs=[pl.BlockSpec((tm, d), lambda i: (i, 0)),
                  pl.BlockSpec((1, d), lambda i: (0, 0)),
                  pl.BlockSpec((d, LANES), lambda i: (0, 0)),
                  pl.BlockSpec((d, LANES), lambda i: (0, 0))],
        out_specs=[pl.BlockSpec((tm, d), lambda i: (i, 0)),
                   pl.BlockSpec((tm, LANES), lambda i: (i, 0))],
        out_shape=[jax.ShapeDtypeStruct((m, d), BF16),
                   jax.ShapeDtypeStruct((m, LANES), F32)],
        compiler_params=_cparams(("parallel",)),
        name="router",
    )(h, g.reshape(1, d), wr_hi, wr_lo)


def _combine_kernel(h_ref, y0_ref, y1_ref, route_ref, g_ref, o_ref, *, final_norm):
    route = route_ref[...]
    w0 = route[:, 0:1]
    w1 = route[:, 1:2]
    out = h_ref[...] + (w0 * y0_ref[...].astype(F32) + w1 * y1_ref[...].astype(F32))
    if final_norm:
        out = _rms(out, g_ref[...])
    o_ref[...] = out


def combine(h, y0, y1, route, g, final_norm):
    m, d = h.shape
    tm = min(ROW_TILE, m)
    rowspec = lambda w: pl.BlockSpec((tm, w), lambda i: (i, 0))
    return pl.pallas_call(
        functools.partial(_combine_kernel, final_norm=final_norm),
        grid=(m // tm,),
        in_specs=[rowspec(d), rowspec(d), rowspec(d), rowspec(LANES),
                  pl.BlockSpec((1, d), lambda i: (0, 0))],
        out_specs=rowspec(d),
        out_shape=jax.ShapeDtypeStruct((m, d), F32),
        compiler_params=_cparams(("parallel",)),
        name="moe_combine",
    )(h, y0, y1, route, g.reshape(1, d))


def _final_norm_kernel(h_ref, g_ref, o_ref):
    o_ref[...] = _rms(h_ref[...], g_ref[...])


def final_norm_rows(h, g):
    m, d = h.shape
    tm = min(ROW_TILE, m)
    return pl.pallas_call(
        _final_norm_kernel,
        grid=(m // tm,),
        in_specs=[pl.BlockSpec((tm, d), lambda i: (i, 0)), pl.BlockSpec((1, d), lambda i: (0, 0))],
        out_specs=pl.BlockSpec((tm, d), lambda i: (i, 0)),
        out_shape=jax.ShapeDtypeStruct((m, d), F32),
        compiler_params=_cparams(("parallel",)),
        name="final_norm",
    )(h, g.reshape(1, d))


def moe_ffn(h, g, w_router, wg, wu, wd, g_final, apply_final_norm):
    m, d = h.shape
    tm = min(MOE_TILE, m)
    u, route = router(h, g, w_router)
    experts = route[:, 2:4].astype(jnp.int32).reshape(-1)
    onehot = (experts[:, None] == jnp.arange(N_EXPERTS)[None, :]).astype(jnp.int32)
    ranks = jnp.cumsum(onehot, axis=0) - onehot
    rank = jnp.sum(ranks * onehot, axis=1)
    counts = jnp.sum(onehot, axis=0)
    tiles_per = (counts + tm - 1) // tm
    tile_end = jnp.cumsum(tiles_per)
    tile_start = tile_end - tiles_per
    pos = tile_start[experts] * tm + rank
    n_rows = TOP_K * m + N_EXPERTS * tm
    n_tiles = n_rows // tm
    row_token = jnp.zeros((n_rows,), jnp.int32).at[pos].set(jnp.arange(TOP_K * m, dtype=jnp.int32) // TOP_K)
    tile_expert = jnp.minimum(
        jnp.sum((jnp.arange(n_tiles)[:, None] >= tile_end[None, :]).astype(jnp.int32), axis=1),
        N_EXPERTS - 1).astype(jnp.int32)
    n_used = tile_end[-1:].astype(jnp.int32)
    x_sorted = jnp.take(u, row_token, axis=0)
    y = expert_ffn(x_sorted, tile_expert, n_used, wg, wu, wd, tm)
    pos2 = pos.reshape(m, TOP_K)
    y0 = jnp.take(y, pos2[:, 0], axis=0)
    y1 = jnp.take(y, pos2[:, 1], axis=0)
    return combine(h, y0, y1, route, g_final, apply_final_norm)


def _in_proj_weight(w):
    o_xbc = SSD_INNER
    o_dt = o_xbc + SSD_CONV_DIM
    o_qkv = o_dt + SSD_HEADS
    o_g = o_qkv + 3 * SB_WIDTH
    dt_cols = jnp.pad(w[:, o_dt:o_qkv], ((0, 0), (0, LANES - SSD_HEADS)))
    return jnp.concatenate([w[:, :o_dt], dt_cols, w[:, o_qkv:]], axis=1).astype(BF16)


def kernel(x, mem, norm_mix, w_in, conv_w, conv_b, dt_bias, a_log, d_skip, ssd_norm, w_ssd_o, w_sb_o, w_out, norm_xa, norm_mem, xa_wq, xa_wk, xa_wv, xa_wo, norm_ffn, ffn_w_gate, ffn_w_up, ffn_w_down, moe_router, moe_w_gate, moe_w_up, moe_w_down, final_norm):
    bsz, s, d = x.shape
    m = bsz * s
    depth = w_in.shape[0]
    mem2 = mem.reshape(-1, d)
    h = x.reshape(m, d)
    segs = (SEG_Z, SEG_XBC, SEG_DT, SEG_Q, SEG_K, SEG_V, SEG_GSSD, SEG_GSB)
    seg_dtypes = (BF16, BF16, F32, BF16, BF16, BF16, BF16, BF16)
    for i in range(depth):
        z, xbc, dt, q, k, v, g_ssd, g_sb = norm_proj(
            h, norm_mix[i], _in_proj_weight(w_in[i]), segs, seg_dtypes, "in_proj")
        sh = lambda a: a.reshape(bsz, s, a.shape[-1])
        y_ssd = ssd_branch(sh(z), sh(xbc), sh(dt), conv_w[i], conv_b[i], dt_bias[i], a_log[i],
                           d_skip[i], ssd_norm[i])
        y_sb = stick_breaking(sh(q), sh(k), sh(v))
        h = merge_mixers(h, y_ssd.reshape(m, -1), y_sb.reshape(m, -1), g_ssd, g_sb,
                         w_ssd_o[i].astype(BF16), w_sb_o[i].astype(BF16), w_out[i].astype(BF16))
        w_kv = jnp.concatenate([xa_wk[i], xa_wv[i]], axis=1).astype(BF16)
        (kv,) = norm_proj(mem2, norm_mem[i], w_kv, ((0, 2 * XA_WIDTH),), (BF16,), "mem_kv")
        h = cross_attention(h.reshape(bsz, s, d), norm_xa[i], xa_wq[i].astype(BF16),
                            kv.reshape(bsz, -1, 2 * XA_WIDTH), xa_wo[i].astype(BF16)).reshape(m, d)
        j = i // 2
        last = i == depth - 1
        if i % 2 == 0:
            h = dense_ffn(h, norm_ffn[i], ffn_w_gate[j].astype(BF16), ffn_w_up[j].astype(BF16),
                          ffn_w_down[j].astype(BF16))
            if last:
                h = final_norm_rows(h, final_norm)
        else:
            h = moe_ffn(h, norm_ffn[i], moe_router[j], moe_w_gate[j].astype(BF16),
                        moe_w_up[j].astype(BF16), moe_w_down[j].astype(BF16), final_norm, last)
    return h.reshape(bsz, s, d)
```

```python
import functools

import jax
import jax.numpy as jnp
from jax import lax
from jax.experimental import pallas as pl
from jax.experimental.pallas import tpu as pltpu

F32 = jnp.float32
BF16 = jnp.bfloat16

D_MODEL = 1024
SSD_HEAD_DIM = 64
SSD_INNER = 1024
SSD_HEADS = 16
SSD_GROUPS = 2
SSD_STATE = 64
SSD_CONV = 4
SSD_CHUNK = 128
SSD_CONV_DIM = SSD_INNER + 2 * SSD_GROUPS * SSD_STATE
SB_HEADS = 8
SB_HEAD_DIM = 64
SB_WIDTH = 512
SB_BLOCK = 128
XA_HEADS = 4
XA_HEAD_DIM = 128
XA_WIDTH = 512
D_FF = 2816
N_EXPERTS = 8
TOP_K = 2
EPS = 1e-6

LANES = 128
VMEM_LIMIT = 56 * 1024 * 1024
ROW_TILE = 512
FF_CHUNK = 256
MOE_TILE = 512

SEG_Z = (0, 1024)
SEG_XBC = (1024, 1280)
SEG_DT = (2304, 128)
SEG_Q = (2432, 512)
SEG_K = (2944, 512)
SEG_V = (3456, 512)
SEG_GSSD = (3968, 1024)
SEG_GSB = (4992, 1024)
IN_PAD_WIDTH = 6016


def _cparams(sem):
    return pltpu.CompilerParams(dimension_semantics=sem, vmem_limit_bytes=VMEM_LIMIT)


def _rms(x, g):
    return x * lax.rsqrt(jnp.mean(x * x, axis=-1, keepdims=True) + EPS) * g


def _split2(x):
    hi = x.astype(BF16)
    lo = (x - hi.astype(F32)).astype(BF16)
    return hi, lo


def _dot(a, b):
    return jnp.dot(a, b, preferred_element_type=F32)


def _dot_nt(a, b):
    return lax.dot_general(a, b, (((1,), (1,)), ((), ())), preferred_element_type=F32)


def _dot_tn(a, b):
    return lax.dot_general(a, b, (((0,), (0,)), ((), ())), preferred_element_type=F32)


def _dot_split(x_f32, m_bf16):
    hi, lo = _split2(x_f32)
    return _dot(hi, m_bf16) + _dot(lo, m_bf16)


def _silu(x):
    return x / (1.0 + jnp.exp(-x))


def _sigmoid(x):
    return 1.0 / (1.0 + jnp.exp(-x))


def _norm_proj_kernel(x_ref, g_ref, w_ref, *out_refs, segs):
    x = x_ref[...]
    u = _rms(x, g_ref[...]).astype(BF16)
    for (start, width), o_ref in zip(segs, out_refs):
        for c0 in range(0, width, 512):
            cw = min(512, width - c0)
            r = _dot(u, w_ref[:, start + c0:start + c0 + cw])
            o_ref[:, c0:c0 + cw] = r.astype(o_ref.dtype)


def norm_proj(x, g, w, segs, dtypes, name):
    m, k = x.shape
    tm = min(ROW_TILE, m)
    n = w.shape[1]
    out_shape = [jax.ShapeDtypeStruct((m, wd), dt) for (_, wd), dt in zip(segs, dtypes)]
    out_specs = [pl.BlockSpec((tm, wd), lambda i: (i, 0)) for (_, wd) in segs]
    return pl.pallas_call(
        functools.partial(_norm_proj_kernel, segs=tuple(segs)),
        grid=(m // tm,),
        in_specs=[pl.BlockSpec((tm, k), lambda i: (i, 0)),
                  pl.BlockSpec((1, k), lambda i: (0, 0)),
                  pl.BlockSpec((k, n), lambda i: (0, 0))],
        out_specs=out_specs,
        out_shape=out_shape,
        compiler_params=_cparams(("parallel",)),
        name=name,
    )(x, g.reshape(1, k), w)


def _ssd_kernel(z_ref, xbc_ref, dt_ref, cw_ref, cb_ref, dtb_ref, aneg_ref, dsk_ref,
                ng_ref, e_ref, o_ref, halo_scr, state_scr, y_scr):
    L = SSD_CHUNK
    c = pl.program_id(1)

    @pl.when(c == 0)
    def _():
        halo_scr[...] = jnp.zeros_like(halo_scr)
        state_scr[...] = jnp.zeros_like(state_scr)

    x_cur = xbc_ref[0].astype(F32)
    halo = halo_scr[...]
    sub = lax.broadcasted_iota(jnp.int32, (8, SSD_CONV_DIM), 0)
    acc = cb_ref[...] + x_cur * cw_ref[SSD_CONV - 1:SSD_CONV, :]
    for d in range(1, SSD_CONV):
        rolled = pltpu.roll(x_cur, d, axis=0)
        head = jnp.where(sub < d, pltpu.roll(halo, d, axis=0), rolled[0:8, :])
        shifted = jnp.concatenate([head, rolled[8:, :]], axis=0)
        acc = acc + shifted * cw_ref[SSD_CONV - 1 - d:SSD_CONV - d, :]
    halo_scr[...] = x_cur[L - 8:L, :]
    xc = _silu(acc)
    xs = xc[:, :SSD_INNER]
    b128 = xc[:, SSD_INNER:SSD_INNER + LANES].astype(BF16)
    c128 = xc[:, SSD_INNER + LANES:SSD_INNER + 2 * LANES]

    row = lax.broadcasted_iota(jnp.int32, (L, L), 0)
    col = lax.broadcasted_iota(jnp.int32, (L, L), 1)
    causal = col <= row
    tril = jnp.where(causal, 1.0, 0.0).astype(BF16)

    dt_in = dt_ref[0] + dtb_ref[...]
    dt = jnp.maximum(dt_in, 0.0) + jnp.log(1.0 + jnp.exp(-jnp.abs(dt_in)))
    a = dt * aneg_ref[...]
    a_hi = a.astype(BF16)
    a_mid = (a - a_hi.astype(F32))
    a_mid_b = a_mid.astype(BF16)
    a_lo = (a_mid - a_mid_b.astype(F32)).astype(BF16)
    acum = _dot(tril, a_hi) + _dot(tril, a_mid_b) + _dot(tril, a_lo)
    acum_t = acum.T
    alast = acum[L - 1:L, :]

    expand = e_ref[...]
    ea_e = _dot_split(jnp.exp(acum), expand)
    w_e = _dot_split(dt * jnp.exp(alast - acum), expand)
    cd_e = _dot_split(jnp.broadcast_to(jnp.exp(alast), (8, LANES)), expand)[0:1, :]
    dt_t = dt.T

    x_b = xs.astype(BF16)
    xw_b = (xs * w_e).astype(BF16)

    lane_half = lax.broadcasted_iota(jnp.int32, (L, LANES), 1) < SSD_STATE
    zero_b = jnp.zeros((L, LANES), BF16)
    c128_b = c128.astype(BF16)
    cb_g = [
        _dot_nt(jnp.where(lane_half, c128_b, zero_b), b128),
        _dot_nt(jnp.where(lane_half, zero_b, c128_b), b128),
    ]
    heads_per_group = SSD_HEADS // SSD_GROUPS
    for p in range(SSD_HEADS // 2):
        xp = x_b[:, p * LANES:(p + 1) * LANES]
        y_pair = None
        for s in range(2):
            h = 2 * p + s
            seg = acum[:, h:h + 1] - acum_t[h:h + 1, :]
            decay = jnp.exp(jnp.where(causal, seg, -jnp.inf))
            m_h = (cb_g[h // heads_per_group] * decay * dt_t[h:h + 1, :]).astype(BF16)
            x_h = jnp.where(lane_half, xp, zero_b) if s == 0 else jnp.where(lane_half, zero_b, xp)
            t = _dot(m_h, x_h)
            y_pair = t if y_pair is None else y_pair + t
        y_scr[:, p * LANES:(p + 1) * LANES] = y_pair

    state = state_scr[...]
    y_off = _dot(c128_b, state.astype(BF16)) * ea_e
    s_new = _dot_tn(b128, xw_b)
    srow = lax.broadcasted_iota(jnp.int32, (LANES, SSD_INNER), 0) < SSD_STATE
    scol = lax.broadcasted_iota(jnp.int32, (LANES, SSD_INNER), 1) < SSD_INNER // SSD_GROUPS
    state_scr[...] = state * cd_e + jnp.where(srow == scol, s_new, 0.0)

    y = y_scr[...] + y_off + xs * dsk_ref[...]
    y = y * _silu(z_ref[0].astype(F32))
    gw = SSD_INNER // SSD_GROUPS
    for g in range(SSD_GROUPS):
        yg = y[:, g * gw:(g + 1) * gw]
        o_ref[0, :, g * gw:(g + 1) * gw] = _rms(yg, ng_ref[:, g * gw:(g + 1) * gw]).astype(o_ref.dtype)


def ssd_branch(z, xbc, dt, conv_w, conv_b, dt_bias, a_log, d_skip, norm_g):
    bsz, s, _ = z.shape
    nc = s // SSD_CHUNK
    pad = LANES - SSD_HEADS
    dtb = jnp.pad(dt_bias.astype(F32), (0, pad)).reshape(1, LANES)
    aneg = jnp.pad(-jnp.exp(a_log.astype(F32)), (0, pad)).reshape(1, LANES)
    dsk = jnp.repeat(d_skip.astype(F32), SSD_HEAD_DIM).reshape(1, SSD_INNER)
    expand = (jnp.arange(LANES)[:, None] == (jnp.arange(SSD_INNER)[None, :] // SSD_HEAD_DIM)).astype(BF16)
    const = lambda shape: pl.BlockSpec(shape, lambda b, c: (0,) * len(shape))
    return pl.pallas_call(
        _ssd_kernel,
        grid=(bsz, nc),
        in_specs=[pl.BlockSpec((1, SSD_CHUNK, SSD_INNER), lambda b, c: (b, c, 0)),
                  pl.BlockSpec((1, SSD_CHUNK, SSD_CONV_DIM), lambda b, c: (b, c, 0)),
                  pl.BlockSpec((1, SSD_CHUNK, LANES), lambda b, c: (b, c, 0)),
                  const((SSD_CONV, SSD_CONV_DIM)),
                  const((1, SSD_CONV_DIM)),
                  const((1, LANES)),
                  const((1, LANES)),
                  const((1, SSD_INNER)),
                  const((1, SSD_INNER)),
                  const((LANES, SSD_INNER))],
        out_specs=pl.BlockSpec((1, SSD_CHUNK, SSD_INNER), lambda b, c: (b, c, 0)),
        out_shape=jax.ShapeDtypeStruct((bsz, s, SSD_INNER), BF16),
        scratch_shapes=[pltpu.VMEM((8, SSD_CONV_DIM), F32),
                        pltpu.VMEM((LANES, SSD_INNER), F32),
                        pltpu.VMEM((SSD_CHUNK, SSD_INNER), F32)],
        compiler_params=_cparams(("parallel", "arbitrary")),
        name="ssd_scan",
    )(z, xbc, dt, conv_w.astype(F32), conv_b.astype(F32).reshape(1, -1), dtb, aneg, dsk,
      norm_g.astype(F32).reshape(1, -1), expand)


SB_PAIRS = SB_WIDTH // LANES
EXP_UNDERFLOW = -104.0


def _sb_kernel(q_ref, k_ref, v_ref, o_ref, run_scr, acc_scr, z_scr):
    T = SB_BLOCK
    qi = pl.program_id(1)
    lane_lo = lax.broadcasted_iota(jnp.int32, (T, LANES), 1) < SB_HEAD_DIM
    zero_b = jnp.zeros((T, LANES), BF16)
    r2 = lax.broadcasted_iota(jnp.int32, (2 * T, 2 * T), 0)
    c2 = lax.broadcasted_iota(jnp.int32, (2 * T, 2 * T), 1)
    nsuffix = jnp.where(((r2 < T) == (c2 < T)) & (r2 > c2), -1.0, 0.0).astype(BF16)
    rw = lax.broadcasted_iota(jnp.int32, (T, 2 * T), 0)
    cw = lax.broadcasted_iota(jnp.int32, (T, 2 * T), 1)
    strict_w = (cw & (T - 1)) < rw
    strict = strict_w[:, :T]
    q_all = q_ref[0] * jnp.asarray(SB_HEAD_DIM ** -0.5, BF16)

    run_scr[...] = jnp.zeros_like(run_scr)
    acc_scr[...] = jnp.zeros_like(acc_scr)
    pairs = range(SB_PAIRS)

    def scores(j):
        kb_all = k_ref[0, pl.ds(pl.multiple_of(j * T, T), T), :]
        for p in pairs:
            kb = kb_all[:, p * LANES:(p + 1) * LANES]
            kcat = jnp.concatenate([jnp.where(lane_lo, kb, zero_b), jnp.where(lane_lo, zero_b, kb)], axis=0)
            z_scr[p] = _dot_nt(q_all[:, p * LANES:(p + 1) * LANES], kcat)

    scores(qi)

    def block(j, diag):
        vb_all = v_ref[0, pl.ds(pl.multiple_of(j * T, T), T), :]
        runmax = None
        sps, lsigs = [], []
        for p in pairs:
            z = z_scr[p]
            sp = jnp.maximum(z, 0.0) + jnp.log(1.0 + jnp.exp(-jnp.abs(z)))
            lsigs.append(z - sp)
            sps.append(jnp.where(strict_w, sp, 0.0) if diag else sp)
        lbs = [_dot(sps[p].astype(BF16), nsuffix) for p in pairs]
        scores(jnp.maximum(j - 1, 0))
        for p in pairs:
            args = lsigs[p] + lbs[p]
            atts = []
            for h in range(2):
                run = run_scr[2 * p + h]
                att = jnp.exp(args[:, h * T:(h + 1) * T] + run)
                if diag:
                    att = jnp.where(strict, att, 0.0)
                atts.append(att.astype(BF16))
                run_new = run[:, 0:1] - jnp.sum(sps[p][:, h * T:(h + 1) * T], axis=-1, keepdims=True)
                run_scr[2 * p + h] = jnp.broadcast_to(run_new, (T, LANES))
                runmax = run_new if runmax is None else jnp.maximum(runmax, run_new)
            vb = vb_all[:, p * LANES:(p + 1) * LANES]
            vcat = jnp.concatenate([jnp.where(lane_lo, vb, zero_b), jnp.where(lane_lo, zero_b, vb)], axis=0)
            acc_scr[p] += _dot(jnp.concatenate(atts, axis=1), vcat)
        return (jnp.max(runmax) > EXP_UNDERFLOW).astype(jnp.int32)

    go = block(qi, True)

    def body(carry):
        i, _ = carry
        return i + 1, block(qi - i, False)

    lax.while_loop(lambda c: (c[0] <= qi) & (c[1] > 0), body, (jnp.int32(1), go))
    for p in range(SB_PAIRS):
        o_ref[0, :, p * LANES:(p + 1) * LANES] = acc_scr[p].astype(o_ref.dtype)


def stick_breaking(q, k, v):
    bsz, s, _ = q.shape
    nq = s // SB_BLOCK
    return pl.pallas_call(
        _sb_kernel,
        grid=(bsz, nq),
        in_specs=[pl.BlockSpec((1, SB_BLOCK, SB_WIDTH), lambda b, i: (b, i, 0)),
                  pl.BlockSpec((1, s, SB_WIDTH), lambda b, i: (b, 0, 0)),
                  pl.BlockSpec((1, s, SB_WIDTH), lambda b, i: (b, 0, 0))],
        out_specs=pl.BlockSpec((1, SB_BLOCK, SB_WIDTH), lambda b, i: (b, i, 0)),
        out_shape=jax.ShapeDtypeStruct((bsz, s, SB_WIDTH), BF16),
        scratch_shapes=[pltpu.VMEM((SB_HEADS, SB_BLOCK, LANES), F32),
                        pltpu.VMEM((SB_PAIRS, SB_BLOCK, LANES), F32),
                        pltpu.VMEM((SB_PAIRS, SB_BLOCK, 2 * SB_BLOCK), F32)],
        compiler_params=_cparams(("parallel", "arbitrary")),
        name="stick_breaking",
    )(q, k, v)


def _merge_kernel(h_ref, ys_ref, yb_ref, gs_ref, gb_ref, ws_ref, wb_ref, wo_ref, o_ref):
    ms = _sigmoid(gs_ref[...].astype(F32)) * _dot(ys_ref[...], ws_ref[...])
    mb = _sigmoid(gb_ref[...].astype(F32)) * _dot(yb_ref[...], wb_ref[...])
    merged = (ms + mb).astype(BF16)
    o_ref[...] = h_ref[...] + _dot(merged, wo_ref[...])


def merge_mixers(h, y_ssd, y_sb, g_ssd, g_sb, w_ssd_o, w_sb_o, w_out):
    m, d = h.shape
    tm = min(ROW_TILE, m)
    rowspec = lambda w: pl.BlockSpec((tm, w), lambda i: (i, 0))
    wspec = lambda r, c: pl.BlockSpec((r, c), lambda i: (0, 0))
    return pl.pallas_call(
        _merge_kernel,
        grid=(m // tm,),
        in_specs=[rowspec(d), rowspec(SSD_INNER), rowspec(SB_WIDTH), rowspec(d), rowspec(d),
                  wspec(SSD_INNER, d), wspec(SB_WIDTH, d), wspec(d, d)],
        out_specs=rowspec(d),
        out_shape=jax.ShapeDtypeStruct((m, d), F32),
        compiler_params=_cparams(("parallel",)),
        name="merge_mixers",
    )(h, y_ssd, y_sb, g_ssd, g_sb, w_ssd_o, w_sb_o, w_out)


def _xattn_kernel(h_ref, g_ref, wq_ref, kv_ref, wo_ref, o_ref):
    h = h_ref[0]
    u = _rms(h, g_ref[...]).astype(BF16)
    q = _dot(u, wq_ref[...]).astype(BF16)
    scale = XA_HEAD_DIM ** -0.5
    outs = []
    for hd in range(XA_HEADS):
        lo = hd * XA_HEAD_DIM
        kh = kv_ref[0, :, lo:lo + XA_HEAD_DIM]
        vh = kv_ref[0, :, XA_WIDTH + lo:XA_WIDTH + lo + XA_HEAD_DIM]
        s = _dot_nt(q[:, lo:lo + XA_HEAD_DIM], kh) * scale
        e = jnp.exp(s - jnp.max(s, axis=-1, keepdims=True))
        oh = _dot(e.astype(BF16), vh) / jnp.sum(e, axis=-1, keepdims=True)
        outs.append(oh.astype(BF16))
    o = jnp.concatenate(outs, axis=-1)
    o_ref[0] = h + _dot(o, wo_ref[...])


def cross_attention(h, g, wq, kv, wo):
    bsz, s, d = h.shape
    tm = min(ROW_TILE, s)
    mem_len = kv.shape[1]
    return pl.pallas_call(
        _xattn_kernel,
        grid=(bsz, s // tm),
        in_specs=[pl.BlockSpec((1, tm, d), lambda b, i: (b, i, 0)),
                  pl.BlockSpec((1, d), lambda b, i: (0, 0)),
                  pl.BlockSpec((d, XA_WIDTH), lambda b, i: (0, 0)),
                  pl.BlockSpec((1, mem_len, 2 * XA_WIDTH), lambda b, i: (b, 0, 0)),
                  pl.BlockSpec((XA_WIDTH, d), lambda b, i: (0, 0))],
        out_specs=pl.BlockSpec((1, tm, d), lambda b, i: (b, i, 0)),
        out_shape=jax.ShapeDtypeStruct((bsz, s, d), F32),
        compiler_params=_cparams(("parallel", "parallel")),
        name="cross_attention",
    )(h, g.reshape(1, d), wq, kv, wo)


def _swiglu_tile(u, wg_ref, wu_ref, wd_ref):
    acc = None
    for c0 in range(0, D_FF, FF_CHUNK):
        gt = _dot(u, wg_ref[0, :, c0:c0 + FF_CHUNK])
        up = _dot(u, wu_ref[0, :, c0:c0 + FF_CHUNK])
        hid = (_silu(gt) * up).astype(BF16)
        t = _dot(hid, wd_ref[0, c0:c0 + FF_CHUNK, :])
        acc = t if acc is None else acc + t
    return acc


def _dense_ffn_kernel(h_ref, g_ref, wg_ref, wu_ref, wd_ref, o_ref):
    h = h_ref[...]
    u = _rms(h, g_ref[...]).astype(BF16)
    o_ref[...] = h + _swiglu_tile(u, wg_ref, wu_ref, wd_ref)


def dense_ffn(h, g, wg, wu, wd, layer):
    m, d = h.shape
    tm = min(ROW_TILE, m)
    return pl.pallas_call(
        _dense_ffn_kernel,
        grid=(m // tm,),
        in_specs=[pl.BlockSpec((tm, d), lambda i: (i, 0)),
                  pl.BlockSpec((1, d), lambda i: (0, 0)),
                  pl.BlockSpec((1, d, D_FF), lambda i: (layer, 0, 0)),
                  pl.BlockSpec((1, d, D_FF), lambda i: (layer, 0, 0)),
                  pl.BlockSpec((1, D_FF, d), lambda i: (layer, 0, 0))],
        out_specs=pl.BlockSpec((tm, d), lambda i: (i, 0)),
        out_shape=jax.ShapeDtypeStruct((m, d), F32),
        compiler_params=_cparams(("parallel",)),
        name="dense_ffn",
    )(h, g.reshape(1, d), wg, wu, wd)


def _expert_ffn_kernel(te_ref, nt_ref, x_ref, wg_ref, wu_ref, wd_ref, o_ref):
    @pl.when(pl.program_id(0) < nt_ref[0])
    def _():
        o_ref[...] = _swiglu_tile(x_ref[...], wg_ref, wu_ref, wd_ref).astype(o_ref.dtype)


def expert_ffn(x_sorted, tile_expert, n_tiles_used, wg, wu, wd, tm):
    r, d = x_sorted.shape
    grid_spec = pltpu.PrefetchScalarGridSpec(
        num_scalar_prefetch=2,
        grid=(r // tm,),
        in_specs=[pl.BlockSpec((tm, d), lambda i, te, nt: (i, 0)),
                  pl.BlockSpec((1, d, D_FF), lambda i, te, nt: (te[i], 0, 0)),
                  pl.BlockSpec((1, d, D_FF), lambda i, te, nt: (te[i], 0, 0)),
                  pl.BlockSpec((1, D_FF, d), lambda i, te, nt: (te[i], 0, 0))],
        out_specs=pl.BlockSpec((tm, d), lambda i, te, nt: (i, 0)),
    )
    return pl.pallas_call(
        _expert_ffn_kernel,
        grid_spec=grid_spec,
        out_shape=jax.ShapeDtypeStruct((r, d), BF16),
        compiler_params=_cparams(("arbitrary",)),
        name="expert_ffn",
    )(tile_expert, n_tiles_used, x_sorted, wg, wu, wd)


def _router_kernel(h_ref, g_ref, wr_hi_ref, wr_lo_ref, u_ref, route_ref, counts_ref, seen_scr):
    @pl.when(pl.program_id(0) == 0)
    def _():
        seen_scr[...] = jnp.zeros_like(seen_scr)

    uf = _rms(h_ref[...], g_ref[...])
    u_hi = uf.astype(BF16)
    u_ref[...] = u_hi
    u_lo = (uf - u_hi.astype(F32)).astype(BF16)
    logits = _dot(u_hi, wr_hi_ref[...]) + (_dot(u_lo, wr_hi_ref[...]) + _dot(u_hi, wr_lo_ref[...]))
    lane = lax.broadcasted_iota(jnp.int32, logits.shape, 1)
    logits = jnp.where(lane < N_EXPERTS, logits, -jnp.inf)
    m1 = jnp.max(logits, axis=-1, keepdims=True)
    i1 = jnp.min(jnp.where(logits == m1, lane, LANES), axis=-1, keepdims=True)
    rest = jnp.where(lane == i1, -jnp.inf, logits)
    m2 = jnp.max(rest, axis=-1, keepdims=True)
    i2 = jnp.min(jnp.where(rest == m2, lane, LANES), axis=-1, keepdims=True)
    e2 = jnp.exp(m2 - m1)
    w1 = 1.0 / (1.0 + e2)
    w2 = e2 / (1.0 + e2)
    route = jnp.where(lane == 0, w1, jnp.where(lane == 1, w2, 0.0))
    route = jnp.where(lane == 2, i1.astype(F32), route)
    route = jnp.where(lane == 3, i2.astype(F32), route)
    tm = logits.shape[0]
    picked = jnp.where((lane == i1) | (lane == i2), 1.0, 0.0)
    r = lax.broadcasted_iota(jnp.int32, (tm, tm), 0)
    c = lax.broadcasted_iota(jnp.int32, (tm, tm), 1)
    earlier = jnp.where(c < r, 1.0, 0.0).astype(BF16)
    before = _dot(earlier, picked.astype(BF16)) + seen_scr[...]
    rank1 = jnp.sum(jnp.where(lane == i1, before, 0.0), axis=-1, keepdims=True)
    rank2 = jnp.sum(jnp.where(lane == i2, before, 0.0), axis=-1, keepdims=True)
    route = jnp.where(lane == 4, rank1, route)
    route = jnp.where(lane == 5, rank2, route)
    route_ref[...] = route
    seen = seen_scr[...] + jnp.sum(picked, axis=0, keepdims=True)
    seen_scr[...] = seen
    counts_ref[...] = jnp.broadcast_to(seen, counts_ref.shape)


def router(h, g, w_router):
    m, d = h.shape
    tm = min(ROW_TILE, m)
    wr = jnp.pad(w_router.astype(F32), ((0, 0), (0, LANES - N_EXPERTS)))
    wr_hi = wr.astype(BF16)
    wr_lo = (wr - wr_hi.astype(F32)).astype(BF16)
    return pl.pallas_call(
        _router_kernel,
        grid=(m // tm,),
        in_specs=[pl.BlockSpec((tm, d), lambda i: (i, 0)),
                  pl.BlockSpec((1, d), lambda i: (0, 0)),
                  pl.BlockSpec((d, LANES), lambda i: (0, 0)),
                  pl.BlockSpec((d, LANES), lambda i: (0, 0))],
        out_specs=[pl.BlockSpec((tm, d), lambda i: (i, 0)),
                   pl.BlockSpec((tm, LANES), lambda i: (i, 0)),
                   pl.BlockSpec((8, LANES), lambda i: (0, 0))],
        out_shape=[jax.ShapeDtypeStruct((m, d), BF16),
                   jax.ShapeDtypeStruct((m, LANES), F32),
                   jax.ShapeDtypeStruct((8, LANES), F32)],
        scratch_shapes=[pltpu.VMEM((1, LANES), F32)],
        compiler_params=_cparams(("arbitrary",)),
        name="router",
    )(h, g.reshape(1, d), wr_hi, wr_lo)


def _combine_kernel(h_ref, y0_ref, y1_ref, route_ref, g_ref, o_ref, *, final_norm):
    route = route_ref[...]
    w0 = route[:, 0:1]
    w1 = route[:, 1:2]
    out = h_ref[...] + (w0 * y0_ref[...].astype(F32) + w1 * y1_ref[...].astype(F32))
    if final_norm:
        out = _rms(out, g_ref[...])
    o_ref[...] = out


def combine(h, y0, y1, route, g, final_norm):
    m, d = h.shape
    tm = min(ROW_TILE, m)
    rowspec = lambda w: pl.BlockSpec((tm, w), lambda i: (i, 0))
    return pl.pallas_call(
        functools.partial(_combine_kernel, final_norm=final_norm),
        grid=(m // tm,),
        in_specs=[rowspec(d), rowspec(d), rowspec(d), rowspec(LANES),
                  pl.BlockSpec((1, d), lambda i: (0, 0))],
        out_specs=rowspec(d),
        out_shape=jax.ShapeDtypeStruct((m, d), F32),
        compiler_params=_cparams(("parallel",)),
        name="moe_combine",
    )(h, y0, y1, route, g.reshape(1, d))


def _final_norm_kernel(h_ref, g_ref, o_ref):
    o_ref[...] = _rms(h_ref[...], g_ref[...])


def final_norm_rows(h, g):
    m, d = h.shape
    tm = min(ROW_TILE, m)
    return pl.pallas_call(
        _final_norm_kernel,
        grid=(m // tm,),
        in_specs=[pl.BlockSpec((tm, d), lambda i: (i, 0)), pl.BlockSpec((1, d), lambda i: (0, 0))],
        out_specs=pl.BlockSpec((tm, d), lambda i: (i, 0)),
        out_shape=jax.ShapeDtypeStruct((m, d), F32),
        compiler_params=_cparams(("parallel",)),
        name="final_norm",
    )(h, g.reshape(1, d))


def moe_ffn(h, g, w_router, wg, wu, wd, layer, g_final, apply_final_norm):
    m, d = h.shape
    tm = min(MOE_TILE, m)
    u, route, counts = router(h, g, w_router)
    counts = counts[0, :N_EXPERTS].astype(jnp.int32)
    tiles_per = (counts + tm - 1) // tm
    tile_end = jnp.cumsum(tiles_per)
    tile_start = tile_end - tiles_per
    experts = route[:, 2:4].astype(jnp.int32)
    rank = route[:, 4:6].astype(jnp.int32)
    onehot = experts[:, :, None] == jnp.arange(N_EXPERTS)[None, None, :]
    pos = jnp.sum(jnp.where(onehot, tile_start[None, None, :], 0), axis=-1) * tm + rank
    n_rows = TOP_K * m + N_EXPERTS * tm
    n_tiles = n_rows // tm
    token = jnp.broadcast_to(jnp.arange(m, dtype=jnp.int32)[:, None], (m, TOP_K))
    row_token = jnp.zeros((n_rows,), jnp.int32).at[pos.reshape(-1)].set(token.reshape(-1))
    tile_expert = jnp.minimum(
        jnp.sum((jnp.arange(n_tiles)[:, None] >= tile_end[None, :]).astype(jnp.int32), axis=1),
        N_EXPERTS - 1).astype(jnp.int32) + layer * N_EXPERTS
    n_used = tile_end[-1:].astype(jnp.int32)
    x_sorted = jnp.take(u, row_token, axis=0)
    y = expert_ffn(x_sorted, tile_expert, n_used, wg, wu, wd, tm)
    y0 = jnp.take(y, pos[:, 0], axis=0)
    y1 = jnp.take(y, pos[:, 1], axis=0)
    return combine(h, y0, y1, route, g_final, apply_final_norm)


def _cast_kernel(x_ref, o_ref):
    o_ref[...] = x_ref[...].astype(o_ref.dtype)


CAST_BLOCK_BYTES = 6 * 1024 * 1024


def cast_bf16(w):
    cols = w.shape[-1]
    rows = w.size // cols
    tr = next(t for t in (4096, 2048, 1024, 512, 256, 128, 64, 32, 16)
              if rows % t == 0 and t * cols * 4 <= CAST_BLOCK_BYTES)
    out = pl.pallas_call(
        _cast_kernel,
        grid=(rows // tr,),
        in_specs=[pl.BlockSpec((tr, cols), lambda i: (i, 0))],
        out_specs=pl.BlockSpec((tr, cols), lambda i: (i, 0)),
        out_shape=jax.ShapeDtypeStruct((rows, cols), BF16),
        compiler_params=_cparams(("parallel",)),
        name="cast_bf16",
    )(w.reshape(rows, cols))
    return out.reshape(w.shape)


def _in_proj_weight(w):
    o_xbc = SSD_INNER
    o_dt = o_xbc + SSD_CONV_DIM
    o_qkv = o_dt + SSD_HEADS
    o_g = o_qkv + 3 * SB_WIDTH
    dt_cols = jnp.pad(w[:, o_dt:o_qkv], ((0, 0), (0, LANES - SSD_HEADS)))
    return jnp.concatenate([w[:, :o_dt], dt_cols, w[:, o_qkv:]], axis=1)


def kernel(x, mem, norm_mix, w_in, conv_w, conv_b, dt_bias, a_log, d_skip, ssd_norm, w_ssd_o, w_sb_o, w_out, norm_xa, norm_mem, xa_wq, xa_wk, xa_wv, xa_wo, norm_ffn, ffn_w_gate, ffn_w_up, ffn_w_down, moe_router, moe_w_gate, moe_w_up, moe_w_down, final_norm):
    bsz, s, d = x.shape
    m = bsz * s
    depth = w_in.shape[0]
    mem2 = mem.reshape(-1, d)
    h = x.reshape(m, d)
    segs = (SEG_Z, SEG_XBC, SEG_DT, SEG_Q, SEG_K, SEG_V, SEG_GSSD, SEG_GSB)
    seg_dtypes = (BF16, BF16, F32, BF16, BF16, BF16, BF16, BF16)
    w_in, w_ssd_o, w_sb_o, w_out, xa_wq, xa_wk, xa_wv, xa_wo = [
        cast_bf16(w) for w in (w_in, w_ssd_o, w_sb_o, w_out, xa_wq, xa_wk, xa_wv, xa_wo)]
    ffn_w = [cast_bf16(w) for w in (ffn_w_gate, ffn_w_up, ffn_w_down)]
    moe_w = [cast_bf16(w).reshape((-1,) + w.shape[2:]) for w in (moe_w_gate, moe_w_up, moe_w_down)]
    for i in range(depth):
        z, xbc, dt, q, k, v, g_ssd, g_sb = norm_proj(
            h, norm_mix[i], _in_proj_weight(w_in[i]), segs, seg_dtypes, "in_proj")
        sh = lambda a: a.reshape(bsz, s, a.shape[-1])
        y_ssd = ssd_branch(sh(z), sh(xbc), sh(dt), conv_w[i], conv_b[i], dt_bias[i], a_log[i],
                           d_skip[i], ssd_norm[i])
        y_sb = stick_breaking(sh(q), sh(k), sh(v))
        h = merge_mixers(h, y_ssd.reshape(m, -1), y_sb.reshape(m, -1), g_ssd, g_sb,
                         w_ssd_o[i], w_sb_o[i], w_out[i])
        w_kv = jnp.concatenate([xa_wk[i], xa_wv[i]], axis=1)
        (kv,) = norm_proj(mem2, norm_mem[i], w_kv, ((0, 2 * XA_WIDTH),), (BF16,), "mem_kv")
        h = cross_attention(h.reshape(bsz, s, d), norm_xa[i], xa_wq[i],
                            kv.reshape(bsz, -1, 2 * XA_WIDTH), xa_wo[i]).reshape(m, d)
        j = i // 2
        last = i == depth - 1
        if i % 2 == 0:
            h = dense_ffn(h, norm_ffn[i], *ffn_w, j)
            if last:
                h = final_norm_rows(h, final_norm)
        else:
            h = moe_ffn(h, norm_ffn[i], moe_router[j], *moe_w, j, final_norm, last)
    return h.reshape(bsz, s, d)
```

```python
import functools

import jax
import jax.numpy as jnp
from jax import lax
from jax.experimental import pallas as pl
from jax.experimental.pallas import tpu as pltpu

F32 = jnp.float32
BF16 = jnp.bfloat16

D_MODEL = 1024
SSD_HEAD_DIM = 64
SSD_INNER = 1024
SSD_HEADS = 16
SSD_GROUPS = 2
SSD_STATE = 64
SSD_CONV = 4
SSD_CHUNK = 128
SSD_CONV_DIM = SSD_INNER + 2 * SSD_GROUPS * SSD_STATE
SB_HEADS = 8
SB_HEAD_DIM = 64
SB_WIDTH = 512
SB_BLOCK = 128
XA_HEADS = 4
XA_HEAD_DIM = 128
XA_WIDTH = 512
D_FF = 2816
N_EXPERTS = 8
TOP_K = 2
EPS = 1e-6

LANES = 128
VMEM_LIMIT = 56 * 1024 * 1024
ROW_TILE = 512
FF_CHUNK = 256
MOE_TILE = 512

SEG_Z = (0, 1024)
SEG_XBC = (1024, 1280)
SEG_DT = (2304, 128)
SEG_Q = (2432, 512)
SEG_K = (2944, 512)
SEG_V = (3456, 512)
SEG_GSSD = (3968, 1024)
SEG_GSB = (4992, 1024)
IN_PAD_WIDTH = 6016


def _cparams(sem):
    return pltpu.CompilerParams(dimension_semantics=sem, vmem_limit_bytes=VMEM_LIMIT)


def _rms(x, g):
    return x * lax.rsqrt(jnp.mean(x * x, axis=-1, keepdims=True) + EPS) * g


def _split2(x):
    hi = x.astype(BF16)
    lo = (x - hi.astype(F32)).astype(BF16)
    return hi, lo


def _dot(a, b):
    return jnp.dot(a, b, preferred_element_type=F32)


def _dot_nt(a, b):
    return lax.dot_general(a, b, (((1,), (1,)), ((), ())), preferred_element_type=F32)


def _dot_tn(a, b):
    return lax.dot_general(a, b, (((0,), (0,)), ((), ())), preferred_element_type=F32)


def _dot_split(x_f32, m_bf16):
    hi, lo = _split2(x_f32)
    return _dot(hi, m_bf16) + _dot(lo, m_bf16)


def _silu(x):
    return x / (1.0 + jnp.exp(-x))


def _sigmoid(x):
    return 1.0 / (1.0 + jnp.exp(-x))


def _norm_proj_kernel(x_ref, g_ref, w_ref, *out_refs, segs):
    x = x_ref[...]
    u = _rms(x, g_ref[...]).astype(BF16)
    for (start, width), o_ref in zip(segs, out_refs):
        for c0 in range(0, width, 512):
            cw = min(512, width - c0)
            r = _dot(u, w_ref[:, start + c0:start + c0 + cw])
            o_ref[:, c0:c0 + cw] = r.astype(o_ref.dtype)


def norm_proj(x, g, w, segs, dtypes, name):
    m, k = x.shape
    tm = min(ROW_TILE, m)
    n = w.shape[1]
    out_shape = [jax.ShapeDtypeStruct((m, wd), dt) for (_, wd), dt in zip(segs, dtypes)]
    out_specs = [pl.BlockSpec((tm, wd), lambda i: (i, 0)) for (_, wd) in segs]
    return pl.pallas_call(
        functools.partial(_norm_proj_kernel, segs=tuple(segs)),
        grid=(m // tm,),
        in_specs=[pl.BlockSpec((tm, k), lambda i: (i, 0)),
                  pl.BlockSpec((1, k), lambda i: (0, 0)),
                  pl.BlockSpec((k, n), lambda i: (0, 0))],
        out_specs=out_specs,
        out_shape=out_shape,
        compiler_params=_cparams(("parallel",)),
        name=name,
    )(x, g.reshape(1, k), w)


def _ssd_kernel(z_ref, xbc_ref, dt_ref, cw_ref, cb_ref, dtb_ref, aneg_ref, dsk_ref,
                ng_ref, e_ref, o_ref, halo_scr, state_scr, y_scr):
    L = SSD_CHUNK
    c = pl.program_id(1)

    @pl.when(c == 0)
    def _():
        halo_scr[...] = jnp.zeros_like(halo_scr)
        state_scr[...] = jnp.zeros_like(state_scr)

    x_cur = xbc_ref[0].astype(F32)
    halo = halo_scr[...]
    sub = lax.broadcasted_iota(jnp.int32, (8, SSD_CONV_DIM), 0)
    acc = cb_ref[...] + x_cur * cw_ref[SSD_CONV - 1:SSD_CONV, :]
    for d in range(1, SSD_CONV):
        rolled = pltpu.roll(x_cur, d, axis=0)
        head = jnp.where(sub < d, pltpu.roll(halo, d, axis=0), rolled[0:8, :])
        shifted = jnp.concatenate([head, rolled[8:, :]], axis=0)
        acc = acc + shifted * cw_ref[SSD_CONV - 1 - d:SSD_CONV - d, :]
    halo_scr[...] = x_cur[L - 8:L, :]
    xc = _silu(acc)
    xs = xc[:, :SSD_INNER]
    b128 = xc[:, SSD_INNER:SSD_INNER + LANES].astype(BF16)
    c128 = xc[:, SSD_INNER + LANES:SSD_INNER + 2 * LANES]

    row = lax.broadcasted_iota(jnp.int32, (L, L), 0)
    col = lax.broadcasted_iota(jnp.int32, (L, L), 1)
    causal = col <= row
    tril = jnp.where(causal, 1.0, 0.0).astype(BF16)

    dt_in = dt_ref[0] + dtb_ref[...]
    dt = jnp.maximum(dt_in, 0.0) + jnp.log(1.0 + jnp.exp(-jnp.abs(dt_in)))
    a = dt * aneg_ref[...]
    a_hi = a.astype(BF16)
    a_mid = (a - a_hi.astype(F32))
    a_mid_b = a_mid.astype(BF16)
    a_lo = (a_mid - a_mid_b.astype(F32)).astype(BF16)
    acum = _dot(tril, a_hi) + _dot(tril, a_mid_b) + _dot(tril, a_lo)
    acum_t = acum.T
    alast = acum[L - 1:L, :]

    expand = e_ref[...]
    ea_e = _dot_split(jnp.exp(acum), expand)
    w_e = _dot_split(dt * jnp.exp(alast - acum), expand)
    cd_e = _dot_split(jnp.broadcast_to(jnp.exp(alast), (8, LANES)), expand)[0:1, :]
    dt_t = dt.T

    x_b = xs.astype(BF16)
    xw_b = (xs * w_e).astype(BF16)

    lane_half = lax.broadcasted_iota(jnp.int32, (L, LANES), 1) < SSD_STATE
    zero_b = jnp.zeros((L, LANES), BF16)
    c128_b = c128.astype(BF16)
    cb_g = [
        _dot_nt(jnp.where(lane_half, c128_b, zero_b), b128),
        _dot_nt(jnp.where(lane_half, zero_b, c128_b), b128),
    ]
    heads_per_group = SSD_HEADS // SSD_GROUPS
    for p in range(SSD_HEADS // 2):
        xp = x_b[:, p * LANES:(p + 1) * LANES]
        y_pair = None
        for s in range(2):
            h = 2 * p + s
            seg = acum[:, h:h + 1] - acum_t[h:h + 1, :]
            decay = jnp.exp(jnp.where(causal, seg, -jnp.inf))
            m_h = (cb_g[h // heads_per_group] * decay * dt_t[h:h + 1, :]).astype(BF16)
            x_h = jnp.where(lane_half, xp, zero_b) if s == 0 else jnp.where(lane_half, zero_b, xp)
            t = _dot(m_h, x_h)
            y_pair = t if y_pair is None else y_pair + t
        y_scr[:, p * LANES:(p + 1) * LANES] = y_pair

    state = state_scr[...]
    y_off = _dot(c128_b, state.astype(BF16)) * ea_e
    s_new = _dot_tn(b128, xw_b)
    srow = lax.broadcasted_iota(jnp.int32, (LANES, SSD_INNER), 0) < SSD_STATE
    scol = lax.broadcasted_iota(jnp.int32, (LANES, SSD_INNER), 1) < SSD_INNER // SSD_GROUPS
    state_scr[...] = state * cd_e + jnp.where(srow == scol, s_new, 0.0)

    y = y_scr[...] + y_off + xs * dsk_ref[...]
    y = y * _silu(z_ref[0].astype(F32))
    gw = SSD_INNER // SSD_GROUPS
    for g in range(SSD_GROUPS):
        yg = y[:, g * gw:(g + 1) * gw]
        o_ref[0, :, g * gw:(g + 1) * gw] = _rms(yg, ng_ref[:, g * gw:(g + 1) * gw]).astype(o_ref.dtype)


def ssd_branch(z, xbc, dt, conv_w, conv_b, dt_bias, a_log, d_skip, norm_g):
    bsz, s, _ = z.shape
    nc = s // SSD_CHUNK
    pad = LANES - SSD_HEADS
    dtb = jnp.pad(dt_bias.astype(F32), (0, pad)).reshape(1, LANES)
    aneg = jnp.pad(-jnp.exp(a_log.astype(F32)), (0, pad)).reshape(1, LANES)
    dsk = jnp.repeat(d_skip.astype(F32), SSD_HEAD_DIM).reshape(1, SSD_INNER)
    expand = (jnp.arange(LANES)[:, None] == (jnp.arange(SSD_INNER)[None, :] // SSD_HEAD_DIM)).astype(BF16)
    const = lambda shape: pl.BlockSpec(shape, lambda b, c: (0,) * len(shape))
    return pl.pallas_call(
        _ssd_kernel,
        grid=(bsz, nc),
        in_specs=[pl.BlockSpec((1, SSD_CHUNK, SSD_INNER), lambda b, c: (b, c, 0)),
                  pl.BlockSpec((1, SSD_CHUNK, SSD_CONV_DIM), lambda b, c: (b, c, 0)),
                  pl.BlockSpec((1, SSD_CHUNK, LANES), lambda b, c: (b, c, 0)),
                  const((SSD_CONV, SSD_CONV_DIM)),
                  const((1, SSD_CONV_DIM)),
                  const((1, LANES)),
                  const((1, LANES)),
                  const((1, SSD_INNER)),
                  const((1, SSD_INNER)),
                  const((LANES, SSD_INNER))],
        out_specs=pl.BlockSpec((1, SSD_CHUNK, SSD_INNER), lambda b, c: (b, c, 0)),
        out_shape=jax.ShapeDtypeStruct((bsz, s, SSD_INNER), BF16),
        scratch_shapes=[pltpu.VMEM((8, SSD_CONV_DIM), F32),
                        pltpu.VMEM((LANES, SSD_INNER), F32),
                        pltpu.VMEM((SSD_CHUNK, SSD_INNER), F32)],
        compiler_params=_cparams(("parallel", "arbitrary")),
        name="ssd_scan",
    )(z, xbc, dt, conv_w.astype(F32), conv_b.astype(F32).reshape(1, -1), dtb, aneg, dsk,
      norm_g.astype(F32).reshape(1, -1), expand)


SB_PAIRS = SB_WIDTH // LANES
EXP_UNDERFLOW = -104.0


def _sb_kernel(q_ref, k_ref, v_ref, o_ref, run_scr, acc_scr, z_scr):
    T = SB_BLOCK
    qi = pl.program_id(1)
    lane_lo = lax.broadcasted_iota(jnp.int32, (T, LANES), 1) < SB_HEAD_DIM
    zero_b = jnp.zeros((T, LANES), BF16)
    r2 = lax.broadcasted_iota(jnp.int32, (2 * T, 2 * T), 0)
    c2 = lax.broadcasted_iota(jnp.int32, (2 * T, 2 * T), 1)
    nsuffix = jnp.where(((r2 < T) == (c2 < T)) & (r2 > c2), -1.0, 0.0).astype(BF16)
    rw = lax.broadcasted_iota(jnp.int32, (T, 2 * T), 0)
    cw = lax.broadcasted_iota(jnp.int32, (T, 2 * T), 1)
    strict_w = (cw & (T - 1)) < rw
    strict = strict_w[:, :T]
    q_all = q_ref[0] * jnp.asarray(SB_HEAD_DIM ** -0.5, BF16)

    run_scr[...] = jnp.zeros_like(run_scr)
    acc_scr[...] = jnp.zeros_like(acc_scr)
    pairs = range(SB_PAIRS)

    def scores(j):
        kb_all = k_ref[0, pl.ds(pl.multiple_of(j * T, T), T), :]
        for p in pairs:
            kb = kb_all[:, p * LANES:(p + 1) * LANES]
            kcat = jnp.concatenate([jnp.where(lane_lo, kb, zero_b), jnp.where(lane_lo, zero_b, kb)], axis=0)
            z_scr[p] = _dot_nt(q_all[:, p * LANES:(p + 1) * LANES], kcat)

    scores(qi)

    def block(j, diag):
        vb_all = v_ref[0, pl.ds(pl.multiple_of(j * T, T), T), :]
        runmax = None
        sps, lsigs = [], []
        for p in pairs:
            z = z_scr[p]
            sp = jnp.maximum(z, 0.0) + jnp.log(1.0 + jnp.exp(-jnp.abs(z)))
            lsigs.append(z - sp)
            sps.append(jnp.where(strict_w, sp, 0.0) if diag else sp)
        lbs = [_dot(sps[p].astype(BF16), nsuffix) for p in pairs]
        scores(jnp.maximum(j - 1, 0))
        for p in pairs:
            args = lsigs[p] + lbs[p]
            atts = []
            for h in range(2):
                run = run_scr[2 * p + h]
                att = jnp.exp(args[:, h * T:(h + 1) * T] + run)
                if diag:
                    att = jnp.where(strict, att, 0.0)
                atts.append(att.astype(BF16))
                run_new = run[:, 0:1] - jnp.sum(sps[p][:, h * T:(h + 1) * T], axis=-1, keepdims=True)
                run_scr[2 * p + h] = jnp.broadcast_to(run_new, (T, LANES))
                runmax = run_new if runmax is None else jnp.maximum(runmax, run_new)
            vb = vb_all[:, p * LANES:(p + 1) * LANES]
            vcat = jnp.concatenate([jnp.where(lane_lo, vb, zero_b), jnp.where(lane_lo, zero_b, vb)], axis=0)
            acc_scr[p] += _dot(jnp.concatenate(atts, axis=1), vcat)
        return (jnp.max(runmax) > EXP_UNDERFLOW).astype(jnp.int32)

    go = block(qi, True)

    def body(carry):
        i, _ = carry
        return i + 1, block(qi - i, False)

    lax.while_loop(lambda c: (c[0] <= qi) & (c[1] > 0), body, (jnp.int32(1), go))
    for p in range(SB_PAIRS):
        o_ref[0, :, p * LANES:(p + 1) * LANES] = acc_scr[p].astype(o_ref.dtype)


def stick_breaking(q, k, v):
    bsz, s, _ = q.shape
    nq = s // SB_BLOCK
    return pl.pallas_call(
        _sb_kernel,
        grid=(bsz, nq),
        in_specs=[pl.BlockSpec((1, SB_BLOCK, SB_WIDTH), lambda b, i: (b, i, 0)),
                  pl.BlockSpec((1, s, SB_WIDTH), lambda b, i: (b, 0, 0)),
                  pl.BlockSpec((1, s, SB_WIDTH), lambda b, i: (b, 0, 0))],
        out_specs=pl.BlockSpec((1, SB_BLOCK, SB_WIDTH), lambda b, i: (b, i, 0)),
        out_shape=jax.ShapeDtypeStruct((bsz, s, SB_WIDTH), BF16),
        scratch_shapes=[pltpu.VMEM((SB_HEADS, SB_BLOCK, LANES), F32),
                        pltpu.VMEM((SB_PAIRS, SB_BLOCK, LANES), F32),
                        pltpu.VMEM((SB_PAIRS, SB_BLOCK, 2 * SB_BLOCK), F32)],
        compiler_params=_cparams(("parallel", "arbitrary")),
        name="stick_breaking",
    )(q, k, v)


def _merge_kernel(h_ref, ys_ref, yb_ref, gs_ref, gb_ref, ws_ref, wb_ref, wo_ref, o_ref):
    ms = _sigmoid(gs_ref[...].astype(F32)) * _dot(ys_ref[...], ws_ref[...])
    mb = _sigmoid(gb_ref[...].astype(F32)) * _dot(yb_ref[...], wb_ref[...])
    merged = (ms + mb).astype(BF16)
    o_ref[...] = h_ref[...] + _dot(merged, wo_ref[...])


def merge_mixers(h, y_ssd, y_sb, g_ssd, g_sb, w_ssd_o, w_sb_o, w_out):
    m, d = h.shape
    tm = min(ROW_TILE, m)
    rowspec = lambda w: pl.BlockSpec((tm, w), lambda i: (i, 0))
    wspec = lambda r, c: pl.BlockSpec((r, c), lambda i: (0, 0))
    return pl.pallas_call(
        _merge_kernel,
        grid=(m // tm,),
        in_specs=[rowspec(d), rowspec(SSD_INNER), rowspec(SB_WIDTH), rowspec(d), rowspec(d),
                  wspec(SSD_INNER, d), wspec(SB_WIDTH, d), wspec(d, d)],
        out_specs=rowspec(d),
        out_shape=jax.ShapeDtypeStruct((m, d), F32),
        compiler_params=_cparams(("parallel",)),
        name="merge_mixers",
    )(h, y_ssd, y_sb, g_ssd, g_sb, w_ssd_o, w_sb_o, w_out)


def _xattn_kernel(h_ref, g_ref, wq_ref, kv_ref, wo_ref, o_ref):
    h = h_ref[0]
    u = _rms(h, g_ref[...]).astype(BF16)
    q = _dot(u, wq_ref[...]).astype(BF16)
    scale = XA_HEAD_DIM ** -0.5
    outs = []
    for hd in range(XA_HEADS):
        lo = hd * XA_HEAD_DIM
        kh = kv_ref[0, :, lo:lo + XA_HEAD_DIM]
        vh = kv_ref[0, :, XA_WIDTH + lo:XA_WIDTH + lo + XA_HEAD_DIM]
        s = _dot_nt(q[:, lo:lo + XA_HEAD_DIM], kh) * scale
        e = jnp.exp(s - jnp.max(s, axis=-1, keepdims=True))
        oh = _dot(e.astype(BF16), vh) / jnp.sum(e, axis=-1, keepdims=True)
        outs.append(oh.astype(BF16))
    o = jnp.concatenate(outs, axis=-1)
    o_ref[0] = h + _dot(o, wo_ref[...])


def cross_attention(h, g, wq, kv, wo):
    bsz, s, d = h.shape
    tm = min(ROW_TILE, s)
    mem_len = kv.shape[1]
    return pl.pallas_call(
        _xattn_kernel,
        grid=(bsz, s // tm),
        in_specs=[pl.BlockSpec((1, tm, d), lambda b, i: (b, i, 0)),
                  pl.BlockSpec((1, d), lambda b, i: (0, 0)),
                  pl.BlockSpec((d, XA_WIDTH), lambda b, i: (0, 0)),
                  pl.BlockSpec((1, mem_len, 2 * XA_WIDTH), lambda b, i: (b, 0, 0)),
                  pl.BlockSpec((XA_WIDTH, d), lambda b, i: (0, 0))],
        out_specs=pl.BlockSpec((1, tm, d), lambda b, i: (b, i, 0)),
        out_shape=jax.ShapeDtypeStruct((bsz, s, d), F32),
        compiler_params=_cparams(("parallel", "parallel")),
        name="cross_attention",
    )(h, g.reshape(1, d), wq, kv, wo)


def _swiglu_tile(u, wg_ref, wu_ref, wd_ref):
    acc = None
    for c0 in range(0, D_FF, FF_CHUNK):
        gt = _dot(u, wg_ref[0, :, c0:c0 + FF_CHUNK])
        up = _dot(u, wu_ref[0, :, c0:c0 + FF_CHUNK])
        hid = (_silu(gt) * up).astype(BF16)
        t = _dot(hid, wd_ref[0, c0:c0 + FF_CHUNK, :])
        acc = t if acc is None else acc + t
    return acc


def _dense_ffn_kernel(h_ref, g_ref, wg_ref, wu_ref, wd_ref, o_ref):
    h = h_ref[...]
    u = _rms(h, g_ref[...]).astype(BF16)
    o_ref[...] = h + _swiglu_tile(u, wg_ref, wu_ref, wd_ref)


def dense_ffn(h, g, wg, wu, wd, layer):
    m, d = h.shape
    tm = min(ROW_TILE, m)
    return pl.pallas_call(
        _dense_ffn_kernel,
        grid=(m // tm,),
        in_specs=[pl.BlockSpec((tm, d), lambda i: (i, 0)),
                  pl.BlockSpec((1, d), lambda i: (0, 0)),
                  pl.BlockSpec((1, d, D_FF), lambda i: (layer, 0, 0)),
                  pl.BlockSpec((1, d, D_FF), lambda i: (layer, 0, 0)),
                  pl.BlockSpec((1, D_FF, d), lambda i: (layer, 0, 0))],
        out_specs=pl.BlockSpec((tm, d), lambda i: (i, 0)),
        out_shape=jax.ShapeDtypeStruct((m, d), F32),
        compiler_params=_cparams(("parallel",)),
        name="dense_ffn",
    )(h, g.reshape(1, d), wg, wu, wd)


def _expert_ffn_kernel(te_ref, nt_ref, x_ref, wg_ref, wu_ref, wd_ref, o_ref):
    @pl.when(pl.program_id(0) < nt_ref[0])
    def _():
        o_ref[...] = _swiglu_tile(x_ref[...], wg_ref, wu_ref, wd_ref).astype(o_ref.dtype)

    @pl.when(pl.program_id(0) >= nt_ref[0])
    def _():
        o_ref[...] = jnp.zeros_like(o_ref)


def expert_ffn(x_sorted, tile_expert, n_tiles_used, wg, wu, wd, tm):
    r, d = x_sorted.shape
    grid_spec = pltpu.PrefetchScalarGridSpec(
        num_scalar_prefetch=2,
        grid=(r // tm,),
        in_specs=[pl.BlockSpec((tm, d), lambda i, te, nt: (i, 0)),
                  pl.BlockSpec((1, d, D_FF), lambda i, te, nt: (te[i], 0, 0)),
                  pl.BlockSpec((1, d, D_FF), lambda i, te, nt: (te[i], 0, 0)),
                  pl.BlockSpec((1, D_FF, d), lambda i, te, nt: (te[i], 0, 0))],
        out_specs=pl.BlockSpec((tm, d), lambda i, te, nt: (i, 0)),
    )
    return pl.pallas_call(
        _expert_ffn_kernel,
        grid_spec=grid_spec,
        out_shape=jax.ShapeDtypeStruct((r, d), BF16),
        compiler_params=_cparams(("arbitrary",)),
        name="expert_ffn",
    )(tile_expert, n_tiles_used, x_sorted, wg, wu, wd)


def _router_kernel(h_ref, g_ref, wr_hi_ref, wr_lo_ref, u_ref, route_ref, fields_ref, counts_ref):
    uf = _rms(h_ref[...], g_ref[...])
    u_hi = uf.astype(BF16)
    u_ref[...] = u_hi
    u_lo = (uf - u_hi.astype(F32)).astype(BF16)
    logits = _dot(u_hi, wr_hi_ref[...]) + (_dot(u_lo, wr_hi_ref[...]) + _dot(u_hi, wr_lo_ref[...]))
    lane = lax.broadcasted_iota(jnp.int32, logits.shape, 1)
    logits = jnp.where(lane < N_EXPERTS, logits, -jnp.inf)
    m1 = jnp.max(logits, axis=-1, keepdims=True)
    i1 = jnp.min(jnp.where(logits == m1, lane, LANES), axis=-1, keepdims=True)
    rest = jnp.where(lane == i1, -jnp.inf, logits)
    m2 = jnp.max(rest, axis=-1, keepdims=True)
    i2 = jnp.min(jnp.where(rest == m2, lane, LANES), axis=-1, keepdims=True)
    e2 = jnp.exp(m2 - m1)
    w1 = 1.0 / (1.0 + e2)
    w2 = e2 / (1.0 + e2)
    route = jnp.where(lane == 0, w1, jnp.where(lane == 1, w2, 0.0))
    route = jnp.where(lane == 2, i1.astype(F32), route)
    route = jnp.where(lane == 3, i2.astype(F32), route)
    tm = logits.shape[0]
    picked = jnp.where(lane == i1, 1.0, jnp.where(lane == i2, 1.0, 0.0))
    r = lax.broadcasted_iota(jnp.int32, (tm, tm), 0)
    c = lax.broadcasted_iota(jnp.int32, (tm, tm), 1)
    earlier = jnp.where(c < r, 1.0, 0.0).astype(BF16)
    before = _dot(earlier, picked.astype(BF16))
    rank1 = jnp.sum(jnp.where(lane == i1, before, 0.0), axis=-1, keepdims=True)
    rank2 = jnp.sum(jnp.where(lane == i2, before, 0.0), axis=-1, keepdims=True)
    route = jnp.where(lane == 4, rank1, route)
    route = jnp.where(lane == 5, rank2, route)
    route_ref[...] = route
    for b in range(tm // LANES):
        fields_ref[:, b * LANES:(b + 1) * LANES] = route[b * LANES:(b + 1) * LANES, :].T[0:8, :]
    counts_ref[...] = jnp.broadcast_to(jnp.sum(picked, axis=0, keepdims=True), counts_ref.shape)


def router(h, g, w_router):
    m, d = h.shape
    tm = min(ROW_TILE, m)
    nt = m // tm
    wr = jnp.pad(w_router.astype(F32), ((0, 0), (0, LANES - N_EXPERTS)))
    wr_hi = wr.astype(BF16)
    wr_lo = (wr - wr_hi.astype(F32)).astype(BF16)
    return pl.pallas_call(
        _router_kernel,
        grid=(nt,),
        in_specs=[pl.BlockSpec((tm, d), lambda i: (i, 0)),
                  pl.BlockSpec((1, d), lambda i: (0, 0)),
                  pl.BlockSpec((d, LANES), lambda i: (0, 0)),
                  pl.BlockSpec((d, LANES), lambda i: (0, 0))],
        out_specs=[pl.BlockSpec((tm, d), lambda i: (i, 0)),
                   pl.BlockSpec((tm, LANES), lambda i: (i, 0)),
                   pl.BlockSpec((8, tm), lambda i: (0, i)),
                   pl.BlockSpec((8, LANES), lambda i: (i, 0))],
        out_shape=[jax.ShapeDtypeStruct((m, d), BF16),
                   jax.ShapeDtypeStruct((m, LANES), F32),
                   jax.ShapeDtypeStruct((8, m), F32),
                   jax.ShapeDtypeStruct((8 * nt, LANES), F32)],
        compiler_params=_cparams(("parallel",)),
        name="router",
    )(h, g.reshape(1, d), wr_hi, wr_lo)


GRAN = 16
LOC_ROWS = TOP_K * ROW_TILE + N_EXPERTS * GRAN
MAX_GRAN = LOC_ROWS // GRAN


def _dispatch_kernel(gdst_ref, ngran_ref, ls_ref, u_ref, f_ref, xin_ref, xout_ref, xs_scr, sem):
    del xin_ref
    t = pl.program_id(0)
    nt = pl.num_programs(0)
    tm = u_ref.shape[0]
    f = f_ref[...]

    def local_row(expert_row, rank_row):
        base = jnp.zeros_like(rank_row)
        for e in range(N_EXPERTS):
            base = jnp.where(expert_row == float(e), ls_ref[t * N_EXPERTS + e].astype(F32), base)
        return (base + rank_row).astype(jnp.int32)

    lp1 = local_row(f[2:3, :], f[4:5, :])
    lp2 = local_row(f[3:4, :], f[5:6, :])
    r = lax.broadcasted_iota(jnp.int32, (LOC_ROWS, tm), 0)
    sel = jnp.where(r == lp1, 1.0, jnp.where(r == lp2, 1.0, 0.0)).astype(BF16)
    xs = _dot(sel, u_ref[...]).astype(BF16)

    def piece(tile, q):
        src = xs_scr.at[pl.ds(pl.multiple_of(q * GRAN, GRAN), GRAN), :]
        dst = xout_ref.at[pl.ds(pl.multiple_of(gdst_ref[tile * MAX_GRAN + q], GRAN), GRAN), :]
        return pltpu.make_async_copy(src, dst, sem.at[0])

    def wait_all(tile):
        lax.fori_loop(0, ngran_ref[tile], lambda q, c: (piece(tile, q).wait(), c)[1], 0)

    @pl.when(t > 0)
    def _():
        wait_all(t - 1)

    xs_scr[...] = xs
    lax.fori_loop(0, ngran_ref[t], lambda q, c: (piece(t, q).start(), c)[1], 0)

    @pl.when(t == nt - 1)
    def _():
        wait_all(t)


def dispatch(u, fields, gdst, ngran, seg_ls, n_rows):
    m, d = u.shape
    tm = min(ROW_TILE, m)
    grid_spec = pltpu.PrefetchScalarGridSpec(
        num_scalar_prefetch=3,
        grid=(m // tm,),
        in_specs=[pl.BlockSpec((tm, d), lambda i, *_: (i, 0)),
                  pl.BlockSpec((8, tm), lambda i, *_: (0, i)),
                  pl.BlockSpec(memory_space=pl.ANY)],
        out_specs=pl.BlockSpec(memory_space=pl.ANY),
        scratch_shapes=[pltpu.VMEM((LOC_ROWS, d), BF16), pltpu.SemaphoreType.DMA((1,))],
    )
    return pl.pallas_call(
        _dispatch_kernel,
        grid_spec=grid_spec,
        out_shape=jax.ShapeDtypeStruct((n_rows, d), BF16),
        input_output_aliases={5: 0},
        compiler_params=_cparams(("arbitrary",)),
        name="moe_dispatch",
    )(gdst, ngran, seg_ls, u, fields, jnp.zeros((n_rows, d), BF16))


def _combine_kernel(gdst_ref, ngran_ref, h_ref, route_ref, ls_ref, g_ref, y_ref, o_ref, y_scr, sem,
                    *, final_norm):
    t = pl.program_id(0)
    tm = h_ref.shape[0]

    @pl.when(t == 0)
    def _():
        y_scr[...] = jnp.zeros_like(y_scr)

    def piece(q):
        src = y_ref.at[pl.ds(pl.multiple_of(gdst_ref[t * MAX_GRAN + q], GRAN), GRAN), :]
        dst = y_scr.at[pl.ds(pl.multiple_of(q * GRAN, GRAN), GRAN), :]
        return pltpu.make_async_copy(src, dst, sem.at[0])

    lax.fori_loop(0, ngran_ref[t], lambda q, c: (piece(q).start(), c)[1], 0)

    route = route_ref[...]
    lane = lax.broadcasted_iota(jnp.int32, (tm, LANES), 1)
    ls_row = ls_ref[0:1, :]

    def local_row(expert_col, rank_col):
        base = jnp.sum(jnp.where(lane == expert_col.astype(jnp.int32), ls_row, 0.0), axis=-1, keepdims=True)
        return (base + rank_col).astype(jnp.int32)

    c = lax.broadcasted_iota(jnp.int32, (tm, LOC_ROWS), 1)
    sel1 = jnp.where(c == local_row(route[:, 2:3], route[:, 4:5]), 1.0, 0.0).astype(BF16)
    sel2 = jnp.where(c == local_row(route[:, 3:4], route[:, 5:6]), 1.0, 0.0).astype(BF16)

    lax.fori_loop(0, ngran_ref[t], lambda q, c_: (piece(q).wait(), c_)[1], 0)
    y = y_scr[...]
    out = h_ref[...] + (route[:, 0:1] * _dot(sel1, y) + route[:, 1:2] * _dot(sel2, y))
    if final_norm:
        out = _rms(out, g_ref[...])
    o_ref[...] = out


def combine(h, y_sorted, route, ls_rows, gdst, ngran, g, final_norm):
    m, d = h.shape
    tm = min(ROW_TILE, m)
    grid_spec = pltpu.PrefetchScalarGridSpec(
        num_scalar_prefetch=2,
        grid=(m // tm,),
        in_specs=[pl.BlockSpec((tm, d), lambda i, *_: (i, 0)),
                  pl.BlockSpec((tm, LANES), lambda i, *_: (i, 0)),
                  pl.BlockSpec((8, LANES), lambda i, *_: (i, 0)),
                  pl.BlockSpec((1, d), lambda i, *_: (0, 0)),
                  pl.BlockSpec(memory_space=pl.ANY)],
        out_specs=pl.BlockSpec((tm, d), lambda i, *_: (i, 0)),
        scratch_shapes=[pltpu.VMEM((LOC_ROWS, d), BF16), pltpu.SemaphoreType.DMA((1,))],
    )
    return pl.pallas_call(
        functools.partial(_combine_kernel, final_norm=final_norm),
        grid_spec=grid_spec,
        out_shape=jax.ShapeDtypeStruct((m, d), F32),
        compiler_params=_cparams(("arbitrary",)),
        name="moe_combine",
    )(gdst, ngran, h, route, ls_rows, g.reshape(1, d), y_sorted)


def _final_norm_kernel(h_ref, g_ref, o_ref):
    o_ref[...] = _rms(h_ref[...], g_ref[...])


def final_norm_rows(h, g):
    m, d = h.shape
    tm = min(ROW_TILE, m)
    return pl.pallas_call(
        _final_norm_kernel,
        grid=(m // tm,),
        in_specs=[pl.BlockSpec((tm, d), lambda i: (i, 0)), pl.BlockSpec((1, d), lambda i: (0, 0))],
        out_specs=pl.BlockSpec((tm, d), lambda i: (i, 0)),
        out_shape=jax.ShapeDtypeStruct((m, d), F32),
        compiler_params=_cparams(("parallel",)),
        name="final_norm",
    )(h, g.reshape(1, d))


def moe_ffn(h, g, w_router, wg, wu, wd, layer, g_final, apply_final_norm):
    m, d = h.shape
    tm = min(MOE_TILE, m)
    nt = m // tm
    u, route, fields, counts = router(h, g, w_router)
    counts = counts.reshape(nt, 8, LANES)[:, 0, :N_EXPERTS].astype(jnp.int32)
    seg = (counts + GRAN - 1) // GRAN * GRAN
    loc_end = jnp.cumsum(seg, axis=1)
    loc_start = loc_end - seg
    group_tiles = (jnp.sum(seg, axis=0) + tm - 1) // tm
    tile_end = jnp.cumsum(group_tiles)
    group_start = (tile_end - group_tiles) * tm
    seg_dst = group_start[None, :] + jnp.cumsum(seg, axis=0) - seg
    q = jnp.arange(MAX_GRAN, dtype=jnp.int32) * GRAN
    piece_expert = jnp.minimum(
        jnp.sum((q[None, :, None] >= loc_end[:, None, :]).astype(jnp.int32), axis=-1), N_EXPERTS - 1)
    gdst = (jnp.take_along_axis(seg_dst, piece_expert, axis=1) + q[None, :]
            - jnp.take_along_axis(loc_start, piece_expert, axis=1)).reshape(-1).astype(jnp.int32)
    ngran = (loc_end[:, -1] // GRAN).astype(jnp.int32)
    n_rows = -(-(TOP_K * m + nt * N_EXPERTS * (GRAN - 1) + N_EXPERTS * (tm - 1)) // tm) * tm
    n_tiles = n_rows // tm
    tile_expert = jnp.minimum(
        jnp.sum((jnp.arange(n_tiles)[:, None] >= tile_end[None, :]).astype(jnp.int32), axis=1),
        N_EXPERTS - 1).astype(jnp.int32) + layer * N_EXPERTS
    n_used = tile_end[-1:].astype(jnp.int32)
    ls_rows = jnp.broadcast_to(
        jnp.pad(loc_start.astype(F32), ((0, 0), (0, LANES - N_EXPERTS)))[:, None, :], (nt, 8, LANES)
    ).reshape(nt * 8, LANES)
    x_sorted = dispatch(u, fields, gdst, ngran, loc_start.reshape(-1).astype(jnp.int32), n_rows)
    y = expert_ffn(x_sorted, tile_expert, n_used, wg, wu, wd, tm)
    return combine(h, y, route, ls_rows, gdst, ngran, g_final, apply_final_norm)


def _cast_kernel(x_ref, o_ref):
    o_ref[...] = x_ref[...].astype(o_ref.dtype)


CAST_BLOCK_BYTES = 6 * 1024 * 1024


def cast_bf16(w):
    cols = w.shape[-1]
    rows = w.size // cols
    tr = next(t for t in (4096, 2048, 1024, 512, 256, 128, 64, 32, 16)
              if rows % t == 0 and t * cols * 4 <= CAST_BLOCK_BYTES)
    out = pl.pallas_call(
        _cast_kernel,
        grid=(rows // tr,),
        in_specs=[pl.BlockSpec((tr, cols), lambda i: (i, 0))],
        out_specs=pl.BlockSpec((tr, cols), lambda i: (i, 0)),
        out_shape=jax.ShapeDtypeStruct((rows, cols), BF16),
        compiler_params=_cparams(("parallel",)),
        name="cast_bf16",
    )(w.reshape(rows, cols))
    return out.reshape(w.shape)


def _in_proj_weight(w):
    o_xbc = SSD_INNER
    o_dt = o_xbc + SSD_CONV_DIM
    o_qkv = o_dt + SSD_HEADS
    o_g = o_qkv + 3 * SB_WIDTH
    dt_cols = jnp.pad(w[:, o_dt:o_qkv], ((0, 0), (0, LANES - SSD_HEADS)))
    return jnp.concatenate([w[:, :o_dt], dt_cols, w[:, o_qkv:]], axis=1)


def kernel(x, mem, norm_mix, w_in, conv_w, conv_b, dt_bias, a_log, d_skip, ssd_norm, w_ssd_o, w_sb_o, w_out, norm_xa, norm_mem, xa_wq, xa_wk, xa_wv, xa_wo, norm_ffn, ffn_w_gate, ffn_w_up, ffn_w_down, moe_router, moe_w_gate, moe_w_up, moe_w_down, final_norm):
    bsz, s, d = x.shape
    m = bsz * s
    depth = w_in.shape[0]
    mem2 = mem.reshape(-1, d)
    h = x.reshape(m, d)
    segs = (SEG_Z, SEG_XBC, SEG_DT, SEG_Q, SEG_K, SEG_V, SEG_GSSD, SEG_GSB)
    seg_dtypes = (BF16, BF16, F32, BF16, BF16, BF16, BF16, BF16)
    w_in, w_ssd_o, w_sb_o, w_out, xa_wq, xa_wk, xa_wv, xa_wo = [
        cast_bf16(w) for w in (w_in, w_ssd_o, w_sb_o, w_out, xa_wq, xa_wk, xa_wv, xa_wo)]
    ffn_w = [cast_bf16(w) for w in (ffn_w_gate, ffn_w_up, ffn_w_down)]
    moe_w = [cast_bf16(w).reshape((-1,) + w.shape[2:]) for w in (moe_w_gate, moe_w_up, moe_w_down)]
    for i in range(depth):
        z, xbc, dt, q, k, v, g_ssd, g_sb = norm_proj(
            h, norm_mix[i], _in_proj_weight(w_in[i]), segs, seg_dtypes, "in_proj")
        sh = lambda a: a.reshape(bsz, s, a.shape[-1])
        y_ssd = ssd_branch(sh(z), sh(xbc), sh(dt), conv_w[i], conv_b[i], dt_bias[i], a_log[i],
                           d_skip[i], ssd_norm[i])
        y_sb = stick_breaking(sh(q), sh(k), sh(v))
        h = merge_mixers(h, y_ssd.reshape(m, -1), y_sb.reshape(m, -1), g_ssd, g_sb,
                         w_ssd_o[i], w_sb_o[i], w_out[i])
        w_kv = jnp.concatenate([xa_wk[i], xa_wv[i]], axis=1)
        (kv,) = norm_proj(mem2, norm_mem[i], w_kv, ((0, 2 * XA_WIDTH),), (BF16,), "mem_kv")
        h = cross_attention(h.reshape(bsz, s, d), norm_xa[i], xa_wq[i],
                            kv.reshape(bsz, -1, 2 * XA_WIDTH), xa_wo[i]).reshape(m, d)
        j = i // 2
        last = i == depth - 1
        if i % 2 == 0:
            h = dense_ffn(h, norm_ffn[i], *ffn_w, j)
            if last:
                h = final_norm_rows(h, final_norm)
        else:
            h = moe_ffn(h, norm_ffn[i], moe_router[j], *moe_w, j, final_norm, last)
    return h.reshape(bsz, s, d)
```

```python
import functools

import jax
import jax.numpy as jnp
from jax import lax
from jax.experimental import pallas as pl
from jax.experimental.pallas import tpu as pltpu

F32 = jnp.float32
BF16 = jnp.bfloat16

D_MODEL = 1024
SSD_HEAD_DIM = 64
SSD_INNER = 1024
SSD_HEADS = 16
SSD_GROUPS = 2
SSD_STATE = 64
SSD_CONV = 4
SSD_CHUNK = 128
SSD_CONV_DIM = SSD_INNER + 2 * SSD_GROUPS * SSD_STATE
SB_HEADS = 8
SB_HEAD_DIM = 64
SB_WIDTH = 512
SB_BLOCK = 128
XA_HEADS = 4
XA_HEAD_DIM = 128
XA_WIDTH = 512
D_FF = 2816
N_EXPERTS = 8
TOP_K = 2
EPS = 1e-6

LANES = 128
VMEM_LIMIT = 56 * 1024 * 1024
ROW_TILE = 512
FF_CHUNK = 256
MOE_TILE = 512

SEG_Z = (0, 1024)
SEG_XBC = (1024, 1280)
SEG_DT = (2304, 128)
SEG_Q = (2432, 512)
SEG_K = (2944, 512)
SEG_V = (3456, 512)
SEG_GSSD = (3968, 1024)
SEG_GSB = (4992, 1024)
IN_PAD_WIDTH = 6016


def _cparams(sem):
    return pltpu.CompilerParams(dimension_semantics=sem, vmem_limit_bytes=VMEM_LIMIT)


def _rms(x, g):
    return x * lax.rsqrt(jnp.mean(x * x, axis=-1, keepdims=True) + EPS) * g


def _split2(x):
    hi = x.astype(BF16)
    lo = (x - hi.astype(F32)).astype(BF16)
    return hi, lo


def _dot(a, b):
    return jnp.dot(a, b, preferred_element_type=F32)


def _dot_nt(a, b):
    return lax.dot_general(a, b, (((1,), (1,)), ((), ())), preferred_element_type=F32)


def _dot_tn(a, b):
    return lax.dot_general(a, b, (((0,), (0,)), ((), ())), preferred_element_type=F32)


def _dot_split(x_f32, m_bf16):
    hi, lo = _split2(x_f32)
    return _dot(hi, m_bf16) + _dot(lo, m_bf16)


def _silu(x):
    return x / (1.0 + jnp.exp(-x))


def _sigmoid(x):
    return 1.0 / (1.0 + jnp.exp(-x))


def _norm_proj_kernel(x_ref, g_ref, w_ref, *out_refs, segs):
    x = x_ref[...]
    u = _rms(x, g_ref[...]).astype(BF16)
    for (start, width), o_ref in zip(segs, out_refs):
        for c0 in range(0, width, 512):
            cw = min(512, width - c0)
            r = _dot(u, w_ref[:, start + c0:start + c0 + cw])
            o_ref[:, c0:c0 + cw] = r.astype(o_ref.dtype)


def norm_proj(x, g, w, segs, dtypes, name):
    m, k = x.shape
    tm = min(ROW_TILE, m)
    n = w.shape[1]
    out_shape = [jax.ShapeDtypeStruct((m, wd), dt) for (_, wd), dt in zip(segs, dtypes)]
    out_specs = [pl.BlockSpec((tm, wd), lambda i: (i, 0)) for (_, wd) in segs]
    return pl.pallas_call(
        functools.partial(_norm_proj_kernel, segs=tuple(segs)),
        grid=(m // tm,),
        in_specs=[pl.BlockSpec((tm, k), lambda i: (i, 0)),
                  pl.BlockSpec((1, k), lambda i: (0, 0)),
                  pl.BlockSpec((k, n), lambda i: (0, 0))],
        out_specs=out_specs,
        out_shape=out_shape,
        compiler_params=_cparams(("parallel",)),
        name=name,
    )(x, g.reshape(1, k), w)


def _ssd_kernel(z_ref, xbc_ref, dt_ref, cw_ref, cb_ref, dtb_ref, aneg_ref, dsk_ref,
                ng_ref, e_ref, o_ref, halo_scr, state_scr, y_scr):
    L = SSD_CHUNK
    c = pl.program_id(1)

    @pl.when(c == 0)
    def _():
        halo_scr[...] = jnp.zeros_like(halo_scr)
        state_scr[...] = jnp.zeros_like(state_scr)

    x_cur = xbc_ref[0].astype(F32)
    halo = halo_scr[...]
    sub = lax.broadcasted_iota(jnp.int32, (8, SSD_CONV_DIM), 0)
    acc = cb_ref[...] + x_cur * cw_ref[SSD_CONV - 1:SSD_CONV, :]
    for d in range(1, SSD_CONV):
        rolled = pltpu.roll(x_cur, d, axis=0)
        head = jnp.where(sub < d, pltpu.roll(halo, d, axis=0), rolled[0:8, :])
        shifted = jnp.concatenate([head, rolled[8:, :]], axis=0)
        acc = acc + shifted * cw_ref[SSD_CONV - 1 - d:SSD_CONV - d, :]
    halo_scr[...] = x_cur[L - 8:L, :]
    xc = _silu(acc)
    xs = xc[:, :SSD_INNER]
    b128 = xc[:, SSD_INNER:SSD_INNER + LANES].astype(BF16)
    c128 = xc[:, SSD_INNER + LANES:SSD_INNER + 2 * LANES]

    row = lax.broadcasted_iota(jnp.int32, (L, L), 0)
    col = lax.broadcasted_iota(jnp.int32, (L, L), 1)
    causal = col <= row
    tril = jnp.where(causal, 1.0, 0.0).astype(BF16)

    dt_in = dt_ref[0] + dtb_ref[...]
    dt = jnp.maximum(dt_in, 0.0) + jnp.log(1.0 + jnp.exp(-jnp.abs(dt_in)))
    a = dt * aneg_ref[...]
    a_hi = a.astype(BF16)
    a_mid = (a - a_hi.astype(F32))
    a_mid_b = a_mid.astype(BF16)
    a_lo = (a_mid - a_mid_b.astype(F32)).astype(BF16)
    acum = _dot(tril, a_hi) + _dot(tril, a_mid_b) + _dot(tril, a_lo)
    acum_t = acum.T
    alast = acum[L - 1:L, :]

    expand = e_ref[...]
    ea_e = _dot_split(jnp.exp(acum), expand)
    w_e = _dot_split(dt * jnp.exp(alast - acum), expand)
    cd_e = _dot_split(jnp.broadcast_to(jnp.exp(alast), (8, LANES)), expand)[0:1, :]
    dt_t = dt.T

    x_b = xs.astype(BF16)
    xw_b = (xs * w_e).astype(BF16)

    lane_half = lax.broadcasted_iota(jnp.int32, (L, LANES), 1) < SSD_STATE
    zero_b = jnp.zeros((L, LANES), BF16)
    c128_b = c128.astype(BF16)
    cb_g = [
        _dot_nt(jnp.where(lane_half, c128_b, zero_b), b128),
        _dot_nt(jnp.where(lane_half, zero_b, c128_b), b128),
    ]
    heads_per_group = SSD_HEADS // SSD_GROUPS
    for p in range(SSD_HEADS // 2):
        xp = x_b[:, p * LANES:(p + 1) * LANES]
        y_pair = None
        for s in range(2):
            h = 2 * p + s
            seg = acum[:, h:h + 1] - acum_t[h:h + 1, :]
            decay = jnp.exp(jnp.where(causal, seg, -jnp.inf))
            m_h = (cb_g[h // heads_per_group] * decay * dt_t[h:h + 1, :]).astype(BF16)
            x_h = jnp.where(lane_half, xp, zero_b) if s == 0 else jnp.where(lane_half, zero_b, xp)
            t = _dot(m_h, x_h)
            y_pair = t if y_pair is None else y_pair + t
        y_scr[:, p * LANES:(p + 1) * LANES] = y_pair

    state = state_scr[...]
    y_off = _dot(c128_b, state.astype(BF16)) * ea_e
    s_new = _dot_tn(b128, xw_b)
    srow = lax.broadcasted_iota(jnp.int32, (LANES, SSD_INNER), 0) < SSD_STATE
    scol = lax.broadcasted_iota(jnp.int32, (LANES, SSD_INNER), 1) < SSD_INNER // SSD_GROUPS
    state_scr[...] = state * cd_e + jnp.where(srow == scol, s_new, 0.0)

    y = y_scr[...] + y_off + xs * dsk_ref[...]
    y = y * _silu(z_ref[0].astype(F32))
    gw = SSD_INNER // SSD_GROUPS
    for g in range(SSD_GROUPS):
        yg = y[:, g * gw:(g + 1) * gw]
        o_ref[0, :, g * gw:(g + 1) * gw] = _rms(yg, ng_ref[:, g * gw:(g + 1) * gw]).astype(o_ref.dtype)


def ssd_branch(z, xbc, dt, conv_w, conv_b, dt_bias, a_log, d_skip, norm_g):
    bsz, s, _ = z.shape
    nc = s // SSD_CHUNK
    pad = LANES - SSD_HEADS
    dtb = jnp.pad(dt_bias.astype(F32), (0, pad)).reshape(1, LANES)
    aneg = jnp.pad(-jnp.exp(a_log.astype(F32)), (0, pad)).reshape(1, LANES)
    dsk = jnp.repeat(d_skip.astype(F32), SSD_HEAD_DIM).reshape(1, SSD_INNER)
    expand = (jnp.arange(LANES)[:, None] == (jnp.arange(SSD_INNER)[None, :] // SSD_HEAD_DIM)).astype(BF16)
    const = lambda shape: pl.BlockSpec(shape, lambda b, c: (0,) * len(shape))
    return pl.pallas_call(
        _ssd_kernel,
        grid=(bsz, nc),
        in_specs=[pl.BlockSpec((1, SSD_CHUNK, SSD_INNER), lambda b, c: (b, c, 0)),
                  pl.BlockSpec((1, SSD_CHUNK, SSD_CONV_DIM), lambda b, c: (b, c, 0)),
                  pl.BlockSpec((1, SSD_CHUNK, LANES), lambda b, c: (b, c, 0)),
                  const((SSD_CONV, SSD_CONV_DIM)),
                  const((1, SSD_CONV_DIM)),
                  const((1, LANES)),
                  const((1, LANES)),
                  const((1, SSD_INNER)),
                  const((1, SSD_INNER)),
                  const((LANES, SSD_INNER))],
        out_specs=pl.BlockSpec((1, SSD_CHUNK, SSD_INNER), lambda b, c: (b, c, 0)),
        out_shape=jax.ShapeDtypeStruct((bsz, s, SSD_INNER), BF16),
        scratch_shapes=[pltpu.VMEM((8, SSD_CONV_DIM), F32),
                        pltpu.VMEM((LANES, SSD_INNER), F32),
                        pltpu.VMEM((SSD_CHUNK, SSD_INNER), F32)],
        compiler_params=_cparams(("parallel", "arbitrary")),
        name="ssd_scan",
    )(z, xbc, dt, conv_w.astype(F32), conv_b.astype(F32).reshape(1, -1), dtb, aneg, dsk,
      norm_g.astype(F32).reshape(1, -1), expand)


SB_PAIRS = SB_WIDTH // LANES
EXP_UNDERFLOW = -104.0


def _sb_kernel(q_ref, k_ref, v_ref, o_ref, run_scr, acc_scr, z_scr, att_scr, kx_scr, vx_scr):
    T = SB_BLOCK
    qi = pl.program_id(1)
    S = k_ref.shape[1]
    r2 = lax.broadcasted_iota(jnp.int32, (2 * T, 2 * T), 0)
    c2 = lax.broadcasted_iota(jnp.int32, (2 * T, 2 * T), 1)
    nsuffix = jnp.where(((r2 < T) == (c2 < T)) & (r2 > c2), -1.0, 0.0).astype(BF16)
    rw = lax.broadcasted_iota(jnp.int32, (T, 2 * T), 0)
    cw = lax.broadcasted_iota(jnp.int32, (T, 2 * T), 1)
    strict_w = (cw & (T - 1)) < rw
    strict = strict_w[:, :T]
    q_all = q_ref[0] * jnp.asarray(SB_HEAD_DIM ** -0.5, BF16)
    pairs = range(SB_PAIRS)

    @pl.when(qi == 0)
    def _():
        rows = 2 * T
        lane_lo = lax.broadcasted_iota(jnp.int32, (rows, LANES), 1) < SB_HEAD_DIM
        zero_b = jnp.zeros((rows, LANES), BF16)
        for r0 in range(0, S, rows):
            for src, dst in ((k_ref, kx_scr), (v_ref, vx_scr)):
                blk = src[0, r0:r0 + rows, :]
                for p in pairs:
                    x = blk[:, p * LANES:(p + 1) * LANES]
                    dst[r0:r0 + rows, 2 * p * LANES:(2 * p + 1) * LANES] = jnp.where(lane_lo, x, zero_b)
                    dst[r0:r0 + rows, (2 * p + 1) * LANES:(2 * p + 2) * LANES] = jnp.where(lane_lo, zero_b, x)

    run_scr[...] = jnp.zeros_like(run_scr)
    acc_scr[...] = jnp.zeros_like(acc_scr)

    def pair_rows(ref, j, p):
        both = ref[pl.ds(pl.multiple_of(j * T, T), T), 2 * p * LANES:(2 * p + 2) * LANES]
        return jnp.concatenate([both[:, :LANES], both[:, LANES:]], axis=0)

    def scores(j):
        for p in pairs:
            z_scr[p] = _dot_nt(q_all[:, p * LANES:(p + 1) * LANES], pair_rows(kx_scr, j, p))

    scores(qi)
    sign = jnp.uint32(0x80000000)

    def weighted_values(j):
        for p in pairs:
            acc_scr[p] += _dot(att_scr[p], pair_rows(vx_scr, j, p))

    def block(j, diag):
        zs = [z_scr[p] for p in pairs]
        scores(jnp.maximum(j - 1, 0))
        if not diag:
            weighted_values(j + 1)
        runmax = None
        sps, lsigs, runs = [], [], []
        for p in pairs:
            z = zs[p]
            neg_abs = lax.bitcast_convert_type(lax.bitcast_convert_type(z, jnp.uint32) | sign, F32)
            sp = jnp.maximum(z, 0.0) + jnp.log(1.0 + jnp.exp(neg_abs))
            lsigs.append(z - sp)
            sp = jnp.where(strict_w, sp, 0.0) if diag else sp
            sps.append(sp)
            for h in range(2):
                run = run_scr[2 * p + h]
                runs.append(run)
                run_new = run[:, 0:1] - jnp.sum(sp[:, h * T:(h + 1) * T], axis=-1, keepdims=True)
                run_scr[2 * p + h] = jnp.broadcast_to(run_new, (T, LANES))
                runmax = run_new if runmax is None else jnp.maximum(runmax, run_new)
        go = (jnp.max(runmax) > EXP_UNDERFLOW).astype(jnp.int32)
        lbs = [_dot(sps[p].astype(BF16), nsuffix) for p in pairs]
        for p in pairs:
            args = lsigs[p] + lbs[p]
            for h in range(2):
                att = jnp.exp(args[:, h * T:(h + 1) * T] + runs[2 * p + h])
                if diag:
                    att = jnp.where(strict, att, 0.0)
                att_scr[p, :, h * T:(h + 1) * T] = att.astype(BF16)
        return go

    go = block(qi, True)

    def body(carry):
        i, _ = carry
        return i + 1, block(qi - i, False)

    n_done, _ = lax.while_loop(lambda c: (c[0] <= qi) & (c[1] > 0), body, (jnp.int32(1), go))
    weighted_values(qi - (n_done - 1))
    for p in range(SB_PAIRS):
        o_ref[0, :, p * LANES:(p + 1) * LANES] = acc_scr[p].astype(o_ref.dtype)


def stick_breaking(q, k, v):
    bsz, s, _ = q.shape
    nq = s // SB_BLOCK
    return pl.pallas_call(
        _sb_kernel,
        grid=(bsz, nq),
        in_specs=[pl.BlockSpec((1, SB_BLOCK, SB_WIDTH), lambda b, i: (b, i, 0)),
                  pl.BlockSpec((1, s, SB_WIDTH), lambda b, i: (b, 0, 0)),
                  pl.BlockSpec((1, s, SB_WIDTH), lambda b, i: (b, 0, 0))],
        out_specs=pl.BlockSpec((1, SB_BLOCK, SB_WIDTH), lambda b, i: (b, i, 0)),
        out_shape=jax.ShapeDtypeStruct((bsz, s, SB_WIDTH), BF16),
        scratch_shapes=[pltpu.VMEM((SB_HEADS, SB_BLOCK, LANES), F32),
                        pltpu.VMEM((SB_PAIRS, SB_BLOCK, LANES), F32),
                        pltpu.VMEM((SB_PAIRS, SB_BLOCK, 2 * SB_BLOCK), F32),
                        pltpu.VMEM((SB_PAIRS, SB_BLOCK, 2 * SB_BLOCK), BF16),
                        pltpu.VMEM((s, 2 * SB_WIDTH), BF16),
                        pltpu.VMEM((s, 2 * SB_WIDTH), BF16)],
        compiler_params=_cparams(("parallel", "arbitrary")),
        name="stick_breaking",
    )(q, k, v)


def _merge_kernel(h_ref, ys_ref, yb_ref, gs_ref, gb_ref, ws_ref, wb_ref, wo_ref, o_ref):
    ms = _sigmoid(gs_ref[...].astype(F32)) * _dot(ys_ref[...], ws_ref[...])
    mb = _sigmoid(gb_ref[...].astype(F32)) * _dot(yb_ref[...], wb_ref[...])
    merged = (ms + mb).astype(BF16)
    o_ref[...] = h_ref[...] + _dot(merged, wo_ref[...])


def merge_mixers(h, y_ssd, y_sb, g_ssd, g_sb, w_ssd_o, w_sb_o, w_out):
    m, d = h.shape
    tm = min(ROW_TILE, m)
    rowspec = lambda w: pl.BlockSpec((tm, w), lambda i: (i, 0))
    wspec = lambda r, c: pl.BlockSpec((r, c), lambda i: (0, 0))
    return pl.pallas_call(
        _merge_kernel,
        grid=(m // tm,),
        in_specs=[rowspec(d), rowspec(SSD_INNER), rowspec(SB_WIDTH), rowspec(d), rowspec(d),
                  wspec(SSD_INNER, d), wspec(SB_WIDTH, d), wspec(d, d)],
        out_specs=rowspec(d),
        out_shape=jax.ShapeDtypeStruct((m, d), F32),
        compiler_params=_cparams(("parallel",)),
        name="merge_mixers",
    )(h, y_ssd, y_sb, g_ssd, g_sb, w_ssd_o, w_sb_o, w_out)


def _xattn_kernel(h_ref, g_ref, wq_ref, kv_ref, wo_ref, o_ref):
    h = h_ref[0]
    u = _rms(h, g_ref[...]).astype(BF16)
    q = _dot(u, wq_ref[...]).astype(BF16)
    scale = XA_HEAD_DIM ** -0.5
    outs = []
    for hd in range(XA_HEADS):
        lo = hd * XA_HEAD_DIM
        kh = kv_ref[0, :, lo:lo + XA_HEAD_DIM]
        vh = kv_ref[0, :, XA_WIDTH + lo:XA_WIDTH + lo + XA_HEAD_DIM]
        s = _dot_nt(q[:, lo:lo + XA_HEAD_DIM], kh) * scale
        e = jnp.exp(s - jnp.max(s, axis=-1, keepdims=True))
        oh = _dot(e.astype(BF16), vh) / jnp.sum(e, axis=-1, keepdims=True)
        outs.append(oh.astype(BF16))
    o = jnp.concatenate(outs, axis=-1)
    o_ref[0] = h + _dot(o, wo_ref[...])


def cross_attention(h, g, wq, kv, wo):
    bsz, s, d = h.shape
    tm = min(ROW_TILE, s)
    mem_len = kv.shape[1]
    return pl.pallas_call(
        _xattn_kernel,
        grid=(bsz, s // tm),
        in_specs=[pl.BlockSpec((1, tm, d), lambda b, i: (b, i, 0)),
                  pl.BlockSpec((1, d), lambda b, i: (0, 0)),
                  pl.BlockSpec((d, XA_WIDTH), lambda b, i: (0, 0)),
                  pl.BlockSpec((1, mem_len, 2 * XA_WIDTH), lambda b, i: (b, 0, 0)),
                  pl.BlockSpec((XA_WIDTH, d), lambda b, i: (0, 0))],
        out_specs=pl.BlockSpec((1, tm, d), lambda b, i: (b, i, 0)),
        out_shape=jax.ShapeDtypeStruct((bsz, s, d), F32),
        compiler_params=_cparams(("parallel", "parallel")),
        name="cross_attention",
    )(h, g.reshape(1, d), wq, kv, wo)


def _swiglu_tile(u, wg_ref, wu_ref, wd_ref):
    acc = None
    for c0 in range(0, D_FF, FF_CHUNK):
        gt = _dot(u, wg_ref[0, :, c0:c0 + FF_CHUNK])
        up = _dot(u, wu_ref[0, :, c0:c0 + FF_CHUNK])
        hid = (_silu(gt) * up).astype(BF16)
        t = _dot(hid, wd_ref[0, c0:c0 + FF_CHUNK, :])
        acc = t if acc is None else acc + t
    return acc


def _dense_ffn_kernel(h_ref, g_ref, wg_ref, wu_ref, wd_ref, o_ref):
    h = h_ref[...]
    u = _rms(h, g_ref[...]).astype(BF16)
    o_ref[...] = h + _swiglu_tile(u, wg_ref, wu_ref, wd_ref)


def dense_ffn(h, g, wg, wu, wd, layer):
    m, d = h.shape
    tm = min(ROW_TILE, m)
    return pl.pallas_call(
        _dense_ffn_kernel,
        grid=(m // tm,),
        in_specs=[pl.BlockSpec((tm, d), lambda i: (i, 0)),
                  pl.BlockSpec((1, d), lambda i: (0, 0)),
                  pl.BlockSpec((1, d, D_FF), lambda i: (layer, 0, 0)),
                  pl.BlockSpec((1, d, D_FF), lambda i: (layer, 0, 0)),
                  pl.BlockSpec((1, D_FF, d), lambda i: (layer, 0, 0))],
        out_specs=pl.BlockSpec((tm, d), lambda i: (i, 0)),
        out_shape=jax.ShapeDtypeStruct((m, d), F32),
        compiler_params=_cparams(("parallel",)),
        name="dense_ffn",
    )(h, g.reshape(1, d), wg, wu, wd)


def _expert_ffn_kernel(te_ref, nt_ref, x_ref, wg_ref, wu_ref, wd_ref, o_ref):
    @pl.when(pl.program_id(0) < nt_ref[0])
    def _():
        o_ref[...] = _swiglu_tile(x_ref[...], wg_ref, wu_ref, wd_ref).astype(o_ref.dtype)

    @pl.when(pl.program_id(0) >= nt_ref[0])
    def _():
        o_ref[...] = jnp.zeros_like(o_ref)


def expert_ffn(x_sorted, tile_expert, n_tiles_used, wg, wu, wd, tm):
    r, d = x_sorted.shape
    grid_spec = pltpu.PrefetchScalarGridSpec(
        num_scalar_prefetch=2,
        grid=(r // tm,),
        in_specs=[pl.BlockSpec((tm, d), lambda i, te, nt: (i, 0)),
                  pl.BlockSpec((1, d, D_FF), lambda i, te, nt: (te[i], 0, 0)),
                  pl.BlockSpec((1, d, D_FF), lambda i, te, nt: (te[i], 0, 0)),
                  pl.BlockSpec((1, D_FF, d), lambda i, te, nt: (te[i], 0, 0))],
        out_specs=pl.BlockSpec((tm, d), lambda i, te, nt: (i, 0)),
    )
    return pl.pallas_call(
        _expert_ffn_kernel,
        grid_spec=grid_spec,
        out_shape=jax.ShapeDtypeStruct((r, d), BF16),
        compiler_params=_cparams(("arbitrary",)),
        name="expert_ffn",
    )(tile_expert, n_tiles_used, x_sorted, wg, wu, wd)


def _router_kernel(h_ref, g_ref, wr_hi_ref, wr_lo_ref, u_ref, route_ref, fields_ref, counts_ref):
    uf = _rms(h_ref[...], g_ref[...])
    u_hi = uf.astype(BF16)
    u_ref[...] = u_hi
    u_lo = (uf - u_hi.astype(F32)).astype(BF16)
    logits = _dot(u_hi, wr_hi_ref[...]) + (_dot(u_lo, wr_hi_ref[...]) + _dot(u_hi, wr_lo_ref[...]))
    lane = lax.broadcasted_iota(jnp.int32, logits.shape, 1)
    logits = jnp.where(lane < N_EXPERTS, logits, -jnp.inf)
    m1 = jnp.max(logits, axis=-1, keepdims=True)
    i1 = jnp.min(jnp.where(logits == m1, lane, LANES), axis=-1, keepdims=True)
    rest = jnp.where(lane == i1, -jnp.inf, logits)
    m2 = jnp.max(rest, axis=-1, keepdims=True)
    i2 = jnp.min(jnp.where(rest == m2, lane, LANES), axis=-1, keepdims=True)
    e2 = jnp.exp(m2 - m1)
    w1 = 1.0 / (1.0 + e2)
    w2 = e2 / (1.0 + e2)
    route = jnp.where(lane == 0, w1, jnp.where(lane == 1, w2, 0.0))
    route = jnp.where(lane == 2, i1.astype(F32), route)
    route = jnp.where(lane == 3, i2.astype(F32), route)
    tm = logits.shape[0]
    picked = jnp.where(lane == i1, 1.0, jnp.where(lane == i2, 1.0, 0.0))
    r = lax.broadcasted_iota(jnp.int32, (tm, tm), 0)
    c = lax.broadcasted_iota(jnp.int32, (tm, tm), 1)
    earlier = jnp.where(c < r, 1.0, 0.0).astype(BF16)
    before = _dot(earlier, picked.astype(BF16))
    rank1 = jnp.sum(jnp.where(lane == i1, before, 0.0), axis=-1, keepdims=True)
    rank2 = jnp.sum(jnp.where(lane == i2, before, 0.0), axis=-1, keepdims=True)
    route = jnp.where(lane == 4, rank1, route)
    route = jnp.where(lane == 5, rank2, route)
    route_ref[...] = route
    for b in range(tm // LANES):
        fields_ref[:, b * LANES:(b + 1) * LANES] = route[b * LANES:(b + 1) * LANES, :].T[0:8, :]
    counts_ref[...] = jnp.broadcast_to(jnp.sum(picked, axis=0, keepdims=True), counts_ref.shape)


def router(h, g, w_router):
    m, d = h.shape
    tm = min(ROW_TILE, m)
    nt = m // tm
    wr = jnp.pad(w_router.astype(F32), ((0, 0), (0, LANES - N_EXPERTS)))
    wr_hi = wr.astype(BF16)
    wr_lo = (wr - wr_hi.astype(F32)).astype(BF16)
    return pl.pallas_call(
        _router_kernel,
        grid=(nt,),
        in_specs=[pl.BlockSpec((tm, d), lambda i: (i, 0)),
                  pl.BlockSpec((1, d), lambda i: (0, 0)),
                  pl.BlockSpec((d, LANES), lambda i: (0, 0)),
                  pl.BlockSpec((d, LANES), lambda i: (0, 0))],
        out_specs=[pl.BlockSpec((tm, d), lambda i: (i, 0)),
                   pl.BlockSpec((tm, LANES), lambda i: (i, 0)),
                   pl.BlockSpec((8, tm), lambda i: (0, i)),
                   pl.BlockSpec((8, LANES), lambda i: (i, 0))],
        out_shape=[jax.ShapeDtypeStruct((m, d), BF16),
                   jax.ShapeDtypeStruct((m, LANES), F32),
                   jax.ShapeDtypeStruct((8, m), F32),
                   jax.ShapeDtypeStruct((8 * nt, LANES), F32)],
        compiler_params=_cparams(("parallel",)),
        name="router",
    )(h, g.reshape(1, d), wr_hi, wr_lo)


GRAN = 16
LOC_ROWS = TOP_K * ROW_TILE + N_EXPERTS * GRAN
MAX_GRAN = LOC_ROWS // GRAN


def _dispatch_kernel(gdst_ref, ngran_ref, ls_ref, u_ref, f_ref, xin_ref, xout_ref, xs_scr, sem):
    del xin_ref
    t = pl.program_id(0)
    nt = pl.num_programs(0)
    tm = u_ref.shape[0]
    f = f_ref[...]

    def local_row(expert_row, rank_row):
        base = jnp.zeros_like(rank_row)
        for e in range(N_EXPERTS):
            base = jnp.where(expert_row == float(e), ls_ref[t * N_EXPERTS + e].astype(F32), base)
        return (base + rank_row).astype(jnp.int32)

    lp1 = local_row(f[2:3, :], f[4:5, :])
    lp2 = local_row(f[3:4, :], f[5:6, :])
    r = lax.broadcasted_iota(jnp.int32, (LOC_ROWS, tm), 0)
    sel = jnp.where(r == lp1, 1.0, jnp.where(r == lp2, 1.0, 0.0)).astype(BF16)
    xs = _dot(sel, u_ref[...]).astype(BF16)

    def piece(tile, q):
        src = xs_scr.at[pl.ds(pl.multiple_of(q * GRAN, GRAN), GRAN), :]
        dst = xout_ref.at[pl.ds(pl.multiple_of(gdst_ref[tile * MAX_GRAN + q], GRAN), GRAN), :]
        return pltpu.make_async_copy(src, dst, sem.at[0])

    def wait_all(tile):
        lax.fori_loop(0, ngran_ref[tile], lambda q, c: (piece(tile, q).wait(), c)[1], 0)

    @pl.when(t > 0)
    def _():
        wait_all(t - 1)

    xs_scr[...] = xs
    lax.fori_loop(0, ngran_ref[t], lambda q, c: (piece(t, q).start(), c)[1], 0)

    @pl.when(t == nt - 1)
    def _():
        wait_all(t)


def dispatch(u, fields, gdst, ngran, seg_ls, n_rows):
    m, d = u.shape
    tm = min(ROW_TILE, m)
    grid_spec = pltpu.PrefetchScalarGridSpec(
        num_scalar_prefetch=3,
        grid=(m // tm,),
        in_specs=[pl.BlockSpec((tm, d), lambda i, *_: (i, 0)),
                  pl.BlockSpec((8, tm), lambda i, *_: (0, i)),
                  pl.BlockSpec(memory_space=pl.ANY)],
        out_specs=pl.BlockSpec(memory_space=pl.ANY),
        scratch_shapes=[pltpu.VMEM((LOC_ROWS, d), BF16), pltpu.SemaphoreType.DMA((1,))],
    )
    return pl.pallas_call(
        _dispatch_kernel,
        grid_spec=grid_spec,
        out_shape=jax.ShapeDtypeStruct((n_rows, d), BF16),
        input_output_aliases={5: 0},
        compiler_params=_cparams(("arbitrary",)),
        name="moe_dispatch",
    )(gdst, ngran, seg_ls, u, fields, jnp.zeros((n_rows, d), BF16))


def _combine_kernel(gdst_ref, ngran_ref, h_ref, route_ref, ls_ref, g_ref, y_ref, o_ref, y_scr, sem,
                    *, final_norm):
    t = pl.program_id(0)
    nt = pl.num_programs(0)
    tm = h_ref.shape[0]
    slot = lax.rem(t, 2)

    def piece(tile, q):
        buf = lax.rem(tile, 2)
        src = y_ref.at[pl.ds(pl.multiple_of(gdst_ref[tile * MAX_GRAN + q], GRAN), GRAN), :]
        dst = y_scr.at[buf, pl.ds(pl.multiple_of(q * GRAN, GRAN), GRAN), :]
        return pltpu.make_async_copy(src, dst, sem.at[buf])

    def fetch(tile):
        lax.fori_loop(0, ngran_ref[tile], lambda q, c: (piece(tile, q).start(), c)[1], 0)

    @pl.when(t == 0)
    def _():
        y_scr[...] = jnp.zeros_like(y_scr)
        fetch(t)

    @pl.when(t + 1 < nt)
    def _():
        fetch(t + 1)

    route = route_ref[...]
    lane = lax.broadcasted_iota(jnp.int32, (tm, LANES), 1)
    ls_row = ls_ref[0:1, :]

    def local_row(expert_col, rank_col):
        base = jnp.sum(jnp.where(lane == expert_col.astype(jnp.int32), ls_row, 0.0), axis=-1, keepdims=True)
        return (base + rank_col).astype(jnp.int32)

    c = lax.broadcasted_iota(jnp.int32, (tm, LOC_ROWS), 1)
    sel1 = jnp.where(c == local_row(route[:, 2:3], route[:, 4:5]), 1.0, 0.0).astype(BF16)
    sel2 = jnp.where(c == local_row(route[:, 3:4], route[:, 5:6]), 1.0, 0.0).astype(BF16)

    lax.fori_loop(0, ngran_ref[t], lambda q, c_: (piece(t, q).wait(), c_)[1], 0)
    y = y_scr[slot]
    out = h_ref[...] + (route[:, 0:1] * _dot(sel1, y) + route[:, 1:2] * _dot(sel2, y))
    if final_norm:
        out = _rms(out, g_ref[...])
    o_ref[...] = out


def combine(h, y_sorted, route, ls_rows, gdst, ngran, g, final_norm):
    m, d = h.shape
    tm = min(ROW_TILE, m)
    grid_spec = pltpu.PrefetchScalarGridSpec(
        num_scalar_prefetch=2,
        grid=(m // tm,),
        in_specs=[pl.BlockSpec((tm, d), lambda i, *_: (i, 0)),
                  pl.BlockSpec((tm, LANES), lambda i, *_: (i, 0)),
                  pl.BlockSpec((8, LANES), lambda i, *_: (i, 0)),
                  pl.BlockSpec((1, d), lambda i, *_: (0, 0)),
                  pl.BlockSpec(memory_space=pl.ANY)],
        out_specs=pl.BlockSpec((tm, d), lambda i, *_: (i, 0)),
        scratch_shapes=[pltpu.VMEM((2, LOC_ROWS, d), BF16), pltpu.SemaphoreType.DMA((2,))],
    )
    return pl.pallas_call(
        functools.partial(_combine_kernel, final_norm=final_norm),
        grid_spec=grid_spec,
        out_shape=jax.ShapeDtypeStruct((m, d), F32),
        compiler_params=_cparams(("arbitrary",)),
        name="moe_combine",
    )(gdst, ngran, h, route, ls_rows, g.reshape(1, d), y_sorted)


def _final_norm_kernel(h_ref, g_ref, o_ref):
    o_ref[...] = _rms(h_ref[...], g_ref[...])


def final_norm_rows(h, g):
    m, d = h.shape
    tm = min(ROW_TILE, m)
    return pl.pallas_call(
        _final_norm_kernel,
        grid=(m // tm,),
        in_specs=[pl.BlockSpec((tm, d), lambda i: (i, 0)), pl.BlockSpec((1, d), lambda i: (0, 0))],
        out_specs=pl.BlockSpec((tm, d), lambda i: (i, 0)),
        out_shape=jax.ShapeDtypeStruct((m, d), F32),
        compiler_params=_cparams(("parallel",)),
        name="final_norm",
    )(h, g.reshape(1, d))


def moe_ffn(h, g, w_router, wg, wu, wd, layer, g_final, apply_final_norm):
    m, d = h.shape
    tm = min(MOE_TILE, m)
    nt = m // tm
    u, route, fields, counts = router(h, g, w_router)
    counts = counts.reshape(nt, 8, LANES)[:, 0, :N_EXPERTS].astype(jnp.int32)
    seg = (counts + GRAN - 1) // GRAN * GRAN
    loc_end = jnp.cumsum(seg, axis=1)
    loc_start = loc_end - seg
    group_tiles = (jnp.sum(seg, axis=0) + tm - 1) // tm
    tile_end = jnp.cumsum(group_tiles)
    group_start = (tile_end - group_tiles) * tm
    seg_dst = group_start[None, :] + jnp.cumsum(seg, axis=0) - seg
    q = jnp.arange(MAX_GRAN, dtype=jnp.int32) * GRAN
    piece_expert = jnp.minimum(
        jnp.sum((q[None, :, None] >= loc_end[:, None, :]).astype(jnp.int32), axis=-1), N_EXPERTS - 1)
    gdst = (jnp.take_along_axis(seg_dst, piece_expert, axis=1) + q[None, :]
            - jnp.take_along_axis(loc_start, piece_expert, axis=1)).reshape(-1).astype(jnp.int32)
    ngran = (loc_end[:, -1] // GRAN).astype(jnp.int32)
    n_rows = -(-(TOP_K * m + nt * N_EXPERTS * (GRAN - 1) + N_EXPERTS * (tm - 1)) // tm) * tm
    n_tiles = n_rows // tm
    tile_expert = jnp.minimum(
        jnp.sum((jnp.arange(n_tiles)[:, None] >= tile_end[None, :]).astype(jnp.int32), axis=1),
        N_EXPERTS - 1).astype(jnp.int32) + layer * N_EXPERTS
    n_used = tile_end[-1:].astype(jnp.int32)
    ls_rows = jnp.broadcast_to(
        jnp.pad(loc_start.astype(F32), ((0, 0), (0, LANES - N_EXPERTS)))[:, None, :], (nt, 8, LANES)
    ).reshape(nt * 8, LANES)
    x_sorted = dispatch(u, fields, gdst, ngran, loc_start.reshape(-1).astype(jnp.int32), n_rows)
    y = expert_ffn(x_sorted, tile_expert, n_used, wg, wu, wd, tm)
    return combine(h, y, route, ls_rows, gdst, ngran, g_final, apply_final_norm)


def _cast_kernel(x_ref, o_ref):
    o_ref[...] = x_ref[...].astype(o_ref.dtype)


CAST_BLOCK_BYTES = 6 * 1024 * 1024


def cast_bf16(w):
    cols = w.shape[-1]
    rows = w.size // cols
    tr = next(t for t in (4096, 2048, 1024, 512, 256, 128, 64, 32, 16)
              if rows % t == 0 and t * cols * 4 <= CAST_BLOCK_BYTES)
    out = pl.pallas_call(
        _cast_kernel,
        grid=(rows // tr,),
        in_specs=[pl.BlockSpec((tr, cols), lambda i: (i, 0))],
        out_specs=pl.BlockSpec((tr, cols), lambda i: (i, 0)),
        out_shape=jax.ShapeDtypeStruct((rows, cols), BF16),
        compiler_params=_cparams(("parallel",)),
        name="cast_bf16",
    )(w.reshape(rows, cols))
    return out.reshape(w.shape)


def _in_proj_weight(w):
    o_xbc = SSD_INNER
    o_dt = o_xbc + SSD_CONV_DIM
    o_qkv = o_dt + SSD_HEADS
    o_g = o_qkv + 3 * SB_WIDTH
    dt_cols = jnp.pad(w[:, o_dt:o_qkv], ((0, 0), (0, LANES - SSD_HEADS)))
    return jnp.concatenate([w[:, :o_dt], dt_cols, w[:, o_qkv:]], axis=1)


def kernel(x, mem, norm_mix, w_in, conv_w, conv_b, dt_bias, a_log, d_skip, ssd_norm, w_ssd_o, w_sb_o, w_out, norm_xa, norm_mem, xa_wq, xa_wk, xa_wv, xa_wo, norm_ffn, ffn_w_gate, ffn_w_up, ffn_w_down, moe_router, moe_w_gate, moe_w_up, moe_w_down, final_norm):
    bsz, s, d = x.shape
    m = bsz * s
    depth = w_in.shape[0]
    mem2 = mem.reshape(-1, d)
    h = x.reshape(m, d)
    segs = (SEG_Z, SEG_XBC, SEG_DT, SEG_Q, SEG_K, SEG_V, SEG_GSSD, SEG_GSB)
    seg_dtypes = (BF16, BF16, F32, BF16, BF16, BF16, BF16, BF16)
    w_in, w_ssd_o, w_sb_o, w_out, xa_wq, xa_wk, xa_wv, xa_wo = [
        cast_bf16(w) for w in (w_in, w_ssd_o, w_sb_o, w_out, xa_wq, xa_wk, xa_wv, xa_wo)]
    ffn_w = [cast_bf16(w) for w in (ffn_w_gate, ffn_w_up, ffn_w_down)]
    moe_w = [cast_bf16(w).reshape((-1,) + w.shape[2:]) for w in (moe_w_gate, moe_w_up, moe_w_down)]
    for i in range(depth):
        z, xbc, dt, q, k, v, g_ssd, g_sb = norm_proj(
            h, norm_mix[i], _in_proj_weight(w_in[i]), segs, seg_dtypes, "in_proj")
        sh = lambda a: a.reshape(bsz, s, a.shape[-1])
        y_ssd = ssd_branch(sh(z), sh(xbc), sh(dt), conv_w[i], conv_b[i], dt_bias[i], a_log[i],
                           d_skip[i], ssd_norm[i])
        y_sb = stick_breaking(sh(q), sh(k), sh(v))
        h = merge_mixers(h, y_ssd.reshape(m, -1), y_sb.reshape(m, -1), g_ssd, g_sb,
                         w_ssd_o[i], w_sb_o[i], w_out[i])
        w_kv = jnp.concatenate([xa_wk[i], xa_wv[i]], axis=1)
        (kv,) = norm_proj(mem2, norm_mem[i], w_kv, ((0, 2 * XA_WIDTH),), (BF16,), "mem_kv")
        h = cross_attention(h.reshape(bsz, s, d), norm_xa[i], xa_wq[i],
                            kv.reshape(bsz, -1, 2 * XA_WIDTH), xa_wo[i]).reshape(m, d)
        j = i // 2
        last = i == depth - 1
        if i % 2 == 0:
            h = dense_ffn(h, norm_ffn[i], *ffn_w, j)
            if last:
                h = final_norm_rows(h, final_norm)
        else:
            h = moe_ffn(h, norm_ffn[i], moe_router[j], *moe_w, j, final_norm, last)
    return h.reshape(bsz, s, d)
```

```python
import functools

import jax
import jax.numpy as jnp
from jax import lax
from jax.experimental import pallas as pl
from jax.experimental.pallas import tpu as pltpu

F32 = jnp.float32
BF16 = jnp.bfloat16

D_MODEL = 1024
SSD_HEAD_DIM = 64
SSD_INNER = 1024
SSD_HEADS = 16
SSD_GROUPS = 2
SSD_STATE = 64
SSD_CONV = 4
SSD_CHUNK = 128
SSD_CONV_DIM = SSD_INNER + 2 * SSD_GROUPS * SSD_STATE
SB_HEADS = 8
SB_HEAD_DIM = 64
SB_WIDTH = 512
SB_BLOCK = 128
XA_HEADS = 4
XA_HEAD_DIM = 128
XA_WIDTH = 512
D_FF = 2816
N_EXPERTS = 8
TOP_K = 2
EPS = 1e-6

LANES = 128
VMEM_LIMIT = 56 * 1024 * 1024
ROW_TILE = 512
FF_CHUNK = 256
MOE_TILE = 512

SEG_Z = (0, 1024)
SEG_XBC = (1024, 1280)
SEG_DT = (2304, 128)
SEG_Q = (2432, 512)
SEG_K = (2944, 512)
SEG_V = (3456, 512)
SEG_GSSD = (3968, 1024)
SEG_GSB = (4992, 1024)
IN_PAD_WIDTH = 6016


def _cparams(sem):
    return pltpu.CompilerParams(dimension_semantics=sem, vmem_limit_bytes=VMEM_LIMIT)


def _rms(x, g):
    return x * lax.rsqrt(jnp.mean(x * x, axis=-1, keepdims=True) + EPS) * g


def _split2(x):
    hi = x.astype(BF16)
    lo = (x - hi.astype(F32)).astype(BF16)
    return hi, lo


def _dot(a, b):
    return jnp.dot(a, b, preferred_element_type=F32)


def _dot_nt(a, b):
    return lax.dot_general(a, b, (((1,), (1,)), ((), ())), preferred_element_type=F32)


def _dot_tn(a, b):
    return lax.dot_general(a, b, (((0,), (0,)), ((), ())), preferred_element_type=F32)


def _dot_split(x_f32, m_bf16):
    hi, lo = _split2(x_f32)
    return _dot(hi, m_bf16) + _dot(lo, m_bf16)


def _silu(x):
    return x / (1.0 + jnp.exp(-x))


def _sigmoid(x):
    return 1.0 / (1.0 + jnp.exp(-x))


def _norm_proj_kernel(x_ref, g_ref, w_ref, *out_refs, segs):
    x = x_ref[...]
    u = _rms(x, g_ref[...]).astype(BF16)
    for (start, width), o_ref in zip(segs, out_refs):
        for c0 in range(0, width, 512):
            cw = min(512, width - c0)
            r = _dot(u, w_ref[:, start + c0:start + c0 + cw])
            o_ref[:, c0:c0 + cw] = r.astype(o_ref.dtype)


def norm_proj(x, g, w, segs, dtypes, name):
    m, k = x.shape
    tm = min(ROW_TILE, m)
    n = w.shape[1]
    out_shape = [jax.ShapeDtypeStruct((m, wd), dt) for (_, wd), dt in zip(segs, dtypes)]
    out_specs = [pl.BlockSpec((tm, wd), lambda i: (i, 0)) for (_, wd) in segs]
    return pl.pallas_call(
        functools.partial(_norm_proj_kernel, segs=tuple(segs)),
        grid=(m // tm,),
        in_specs=[pl.BlockSpec((tm, k), lambda i: (i, 0)),
                  pl.BlockSpec((1, k), lambda i: (0, 0)),
                  pl.BlockSpec((k, n), lambda i: (0, 0))],
        out_specs=out_specs,
        out_shape=out_shape,
        compiler_params=_cparams(("parallel",)),
        name=name,
    )(x, g.reshape(1, k), w)


def _ssd_kernel(z_ref, xbc_ref, dt_ref, cw_ref, cb_ref, dtb_ref, aneg_ref, dsk_ref,
                ng_ref, e_ref, o_ref, halo_scr, state_scr, y_scr):
    L = SSD_CHUNK
    c = pl.program_id(1)

    @pl.when(c == 0)
    def _():
        halo_scr[...] = jnp.zeros_like(halo_scr)
        state_scr[...] = jnp.zeros_like(state_scr)

    x_cur = xbc_ref[0].astype(F32)
    halo = halo_scr[...]
    sub = lax.broadcasted_iota(jnp.int32, (8, SSD_CONV_DIM), 0)
    acc = cb_ref[...] + x_cur * cw_ref[SSD_CONV - 1:SSD_CONV, :]
    for d in range(1, SSD_CONV):
        rolled = pltpu.roll(x_cur, d, axis=0)
        head = jnp.where(sub < d, pltpu.roll(halo, d, axis=0), rolled[0:8, :])
        shifted = jnp.concatenate([head, rolled[8:, :]], axis=0)
        acc = acc + shifted * cw_ref[SSD_CONV - 1 - d:SSD_CONV - d, :]
    halo_scr[...] = x_cur[L - 8:L, :]
    xc = _silu(acc)
    xs = xc[:, :SSD_INNER]
    b128 = xc[:, SSD_INNER:SSD_INNER + LANES].astype(BF16)
    c128 = xc[:, SSD_INNER + LANES:SSD_INNER + 2 * LANES]

    row = lax.broadcasted_iota(jnp.int32, (L, L), 0)
    col = lax.broadcasted_iota(jnp.int32, (L, L), 1)
    causal = col <= row
    tril = jnp.where(causal, 1.0, 0.0).astype(BF16)

    dt_in = dt_ref[0] + dtb_ref[...]
    dt = jnp.maximum(dt_in, 0.0) + jnp.log(1.0 + jnp.exp(-jnp.abs(dt_in)))
    a = dt * aneg_ref[...]
    a_hi = a.astype(BF16)
    a_mid = (a - a_hi.astype(F32))
    a_mid_b = a_mid.astype(BF16)
    a_lo = (a_mid - a_mid_b.astype(F32)).astype(BF16)
    acum = _dot(tril, a_hi) + _dot(tril, a_mid_b) + _dot(tril, a_lo)
    acum_t = acum.T
    alast = acum[L - 1:L, :]

    expand = e_ref[...]
    ea_e = _dot_split(jnp.exp(acum), expand)
    w_e = _dot_split(dt * jnp.exp(alast - acum), expand)
    cd_e = _dot_split(jnp.broadcast_to(jnp.exp(alast), (8, LANES)), expand)[0:1, :]
    dt_t = dt.T

    x_b = xs.astype(BF16)
    xw_b = (xs * w_e).astype(BF16)

    lane_half = lax.broadcasted_iota(jnp.int32, (L, LANES), 1) < SSD_STATE
    zero_b = jnp.zeros((L, LANES), BF16)
    c128_b = c128.astype(BF16)
    cb_g = [
        _dot_nt(jnp.where(lane_half, c128_b, zero_b), b128),
        _dot_nt(jnp.where(lane_half, zero_b, c128_b), b128),
    ]
    heads_per_group = SSD_HEADS // SSD_GROUPS
    for p in range(SSD_HEADS // 2):
        xp = x_b[:, p * LANES:(p + 1) * LANES]
        y_pair = None
        for s in range(2):
            h = 2 * p + s
            seg = acum[:, h:h + 1] - acum_t[h:h + 1, :]
            decay = jnp.exp(jnp.where(causal, seg, -jnp.inf))
            m_h = (cb_g[h // heads_per_group] * decay * dt_t[h:h + 1, :]).astype(BF16)
            x_h = jnp.where(lane_half, xp, zero_b) if s == 0 else jnp.where(lane_half, zero_b, xp)
            t = _dot(m_h, x_h)
            y_pair = t if y_pair is None else y_pair + t
        y_scr[:, p * LANES:(p + 1) * LANES] = y_pair

    state = state_scr[...]
    y_off = _dot(c128_b, state.astype(BF16)) * ea_e
    s_new = _dot_tn(b128, xw_b)
    srow = lax.broadcasted_iota(jnp.int32, (LANES, SSD_INNER), 0) < SSD_STATE
    scol = lax.broadcasted_iota(jnp.int32, (LANES, SSD_INNER), 1) < SSD_INNER // SSD_GROUPS
    state_scr[...] = state * cd_e + jnp.where(srow == scol, s_new, 0.0)

    y = y_scr[...] + y_off + xs * dsk_ref[...]
    y = y * _silu(z_ref[0].astype(F32))
    gw = SSD_INNER // SSD_GROUPS
    for g in range(SSD_GROUPS):
        yg = y[:, g * gw:(g + 1) * gw]
        o_ref[0, :, g * gw:(g + 1) * gw] = _rms(yg, ng_ref[:, g * gw:(g + 1) * gw]).astype(o_ref.dtype)


def ssd_branch(z, xbc, dt, conv_w, conv_b, dt_bias, a_log, d_skip, norm_g):
    bsz, s, _ = z.shape
    nc = s // SSD_CHUNK
    pad = LANES - SSD_HEADS
    dtb = jnp.pad(dt_bias.astype(F32), (0, pad)).reshape(1, LANES)
    aneg = jnp.pad(-jnp.exp(a_log.astype(F32)), (0, pad)).reshape(1, LANES)
    dsk = jnp.repeat(d_skip.astype(F32), SSD_HEAD_DIM).reshape(1, SSD_INNER)
    expand = (jnp.arange(LANES)[:, None] == (jnp.arange(SSD_INNER)[None, :] // SSD_HEAD_DIM)).astype(BF16)
    const = lambda shape: pl.BlockSpec(shape, lambda b, c: (0,) * len(shape))
    return pl.pallas_call(
        _ssd_kernel,
        grid=(bsz, nc),
        in_specs=[pl.BlockSpec((1, SSD_CHUNK, SSD_INNER), lambda b, c: (b, c, 0)),
                  pl.BlockSpec((1, SSD_CHUNK, SSD_CONV_DIM), lambda b, c: (b, c, 0)),
                  pl.BlockSpec((1, SSD_CHUNK, LANES), lambda b, c: (b, c, 0)),
                  const((SSD_CONV, SSD_CONV_DIM)),
                  const((1, SSD_CONV_DIM)),
                  const((1, LANES)),
                  const((1, LANES)),
                  const((1, SSD_INNER)),
                  const((1, SSD_INNER)),
                  const((LANES, SSD_INNER))],
        out_specs=pl.BlockSpec((1, SSD_CHUNK, SSD_INNER), lambda b, c: (b, c, 0)),
        out_shape=jax.ShapeDtypeStruct((bsz, s, SSD_INNER), BF16),
        scratch_shapes=[pltpu.VMEM((8, SSD_CONV_DIM), F32),
                        pltpu.VMEM((LANES, SSD_INNER), F32),
                        pltpu.VMEM((SSD_CHUNK, SSD_INNER), F32)],
        compiler_params=_cparams(("parallel", "arbitrary")),
        name="ssd_scan",
    )(z, xbc, dt, conv_w.astype(F32), conv_b.astype(F32).reshape(1, -1), dtb, aneg, dsk,
      norm_g.astype(F32).reshape(1, -1), expand)


SB_PAIRS = SB_WIDTH // LANES
EXP_UNDERFLOW = -104.0


def _sb_kernel(q_ref, k_ref, v_ref, nsuffix_ref, o_ref, run_scr, acc_scr, z_scr):
    T = SB_BLOCK
    qi = pl.program_id(1)
    nsuffix = nsuffix_ref[...]
    rw = lax.broadcasted_iota(jnp.int32, (T, 2 * T), 0)
    cw = lax.broadcasted_iota(jnp.int32, (T, 2 * T), 1)
    strict_w = (cw & (T - 1)) < rw
    strict = strict_w[:, :T]
    q_all = q_ref[0] * jnp.asarray(SB_HEAD_DIM ** -0.5, BF16)
    pairs = range(SB_PAIRS)
    lane_lo = lax.broadcasted_iota(jnp.int32, (T, LANES), 1) < SB_HEAD_DIM
    zero_b = jnp.zeros((T, LANES), BF16)

    run_scr[...] = jnp.zeros_like(run_scr)
    acc_scr[...] = jnp.zeros_like(acc_scr)

    def pair_rows(ref, j, p):
        x = ref[0, pl.ds(pl.multiple_of(j * T, T), T), p * LANES:(p + 1) * LANES]
        return jnp.concatenate([jnp.where(lane_lo, x, zero_b), jnp.where(lane_lo, zero_b, x)], axis=0)

    def scores(j):
        for p in pairs:
            z_scr[p] = _dot_nt(q_all[:, p * LANES:(p + 1) * LANES], pair_rows(k_ref, j, p))

    scores(qi)

    def block(j, diag):
        runmax = None
        sps, lsigs = [], []
        for p in pairs:
            z = z_scr[p]
            sp = jnp.maximum(z, 0.0) + jnp.log(1.0 + jnp.exp(-jnp.abs(z)))
            lsigs.append(z - sp)
            sps.append(jnp.where(strict_w, sp, 0.0) if diag else sp)
        lbs = [_dot(sps[p].astype(BF16), nsuffix) for p in pairs]
        scores(jnp.maximum(j - 1, 0))
        for p in pairs:
            args = lsigs[p] + lbs[p]
            atts = []
            for h in range(2):
                run = run_scr[2 * p + h]
                att = jnp.exp(args[:, h * T:(h + 1) * T] + run)
                if diag:
                    att = jnp.where(strict, att, 0.0)
                atts.append(att.astype(BF16))
                run_new = run[:, 0:1] - jnp.sum(sps[p][:, h * T:(h + 1) * T], axis=-1, keepdims=True)
                run_scr[2 * p + h] = jnp.broadcast_to(run_new, (T, LANES))
                runmax = run_new if runmax is None else jnp.maximum(runmax, run_new)
            acc_scr[p] += _dot(jnp.concatenate(atts, axis=1), pair_rows(v_ref, j, p))
        return (jnp.max(runmax) > EXP_UNDERFLOW).astype(jnp.int32)

    go = block(qi, True)

    def body(carry):
        i, _ = carry
        return i + 1, block(qi - i, False)

    lax.while_loop(lambda c: (c[0] <= qi) & (c[1] > 0), body, (jnp.int32(1), go))
    for p in range(SB_PAIRS):
        o_ref[0, :, p * LANES:(p + 1) * LANES] = acc_scr[p].astype(o_ref.dtype)


def stick_breaking(q, k, v):
    bsz, s, _ = q.shape
    nq = s // SB_BLOCK
    idx = jnp.arange(2 * SB_BLOCK)
    same_head = (idx[:, None] < SB_BLOCK) == (idx[None, :] < SB_BLOCK)
    nsuffix = jnp.where(same_head & (idx[:, None] > idx[None, :]), -1.0, 0.0).astype(BF16)
    return pl.pallas_call(
        _sb_kernel,
        grid=(bsz, nq),
        in_specs=[pl.BlockSpec((1, SB_BLOCK, SB_WIDTH), lambda b, i: (b, i, 0)),
                  pl.BlockSpec((1, s, SB_WIDTH), lambda b, i: (b, 0, 0)),
                  pl.BlockSpec((1, s, SB_WIDTH), lambda b, i: (b, 0, 0)),
                  pl.BlockSpec((2 * SB_BLOCK, 2 * SB_BLOCK), lambda b, i: (0, 0))],
        out_specs=pl.BlockSpec((1, SB_BLOCK, SB_WIDTH), lambda b, i: (b, i, 0)),
        out_shape=jax.ShapeDtypeStruct((bsz, s, SB_WIDTH), BF16),
        scratch_shapes=[pltpu.VMEM((SB_HEADS, SB_BLOCK, LANES), F32),
                        pltpu.VMEM((SB_PAIRS, SB_BLOCK, LANES), F32),
                        pltpu.VMEM((SB_PAIRS, SB_BLOCK, 2 * SB_BLOCK), F32)],
        compiler_params=_cparams(("parallel", "arbitrary")),
        name="stick_breaking",
    )(q, k, v, nsuffix)


def _merge_kernel(h_ref, ys_ref, yb_ref, gs_ref, gb_ref, ws_ref, wb_ref, wo_ref, o_ref):
    ms = _sigmoid(gs_ref[...].astype(F32)) * _dot(ys_ref[...], ws_ref[...])
    mb = _sigmoid(gb_ref[...].astype(F32)) * _dot(yb_ref[...], wb_ref[...])
    merged = (ms + mb).astype(BF16)
    o_ref[...] = h_ref[...] + _dot(merged, wo_ref[...])


def merge_mixers(h, y_ssd, y_sb, g_ssd, g_sb, w_ssd_o, w_sb_o, w_out):
    m, d = h.shape
    tm = min(ROW_TILE, m)
    rowspec = lambda w: pl.BlockSpec((tm, w), lambda i: (i, 0))
    wspec = lambda r, c: pl.BlockSpec((r, c), lambda i: (0, 0))
    return pl.pallas_call(
        _merge_kernel,
        grid=(m // tm,),
        in_specs=[rowspec(d), rowspec(SSD_INNER), rowspec(SB_WIDTH), rowspec(d), rowspec(d),
                  wspec(SSD_INNER, d), wspec(SB_WIDTH, d), wspec(d, d)],
        out_specs=rowspec(d),
        out_shape=jax.ShapeDtypeStruct((m, d), F32),
        compiler_params=_cparams(("parallel",)),
        name="merge_mixers",
    )(h, y_ssd, y_sb, g_ssd, g_sb, w_ssd_o, w_sb_o, w_out)


def _xattn_kernel(h_ref, g_ref, wq_ref, kv_ref, wo_ref, o_ref):
    h = h_ref[0]
    u = _rms(h, g_ref[...]).astype(BF16)
    q = _dot(u, wq_ref[...]).astype(BF16)
    scale = XA_HEAD_DIM ** -0.5
    outs = []
    for hd in range(XA_HEADS):
        lo = hd * XA_HEAD_DIM
        kh = kv_ref[0, :, lo:lo + XA_HEAD_DIM]
        vh = kv_ref[0, :, XA_WIDTH + lo:XA_WIDTH + lo + XA_HEAD_DIM]
        s = _dot_nt(q[:, lo:lo + XA_HEAD_DIM], kh) * scale
        e = jnp.exp(s - jnp.max(s, axis=-1, keepdims=True))
        oh = _dot(e.astype(BF16), vh) / jnp.sum(e, axis=-1, keepdims=True)
        outs.append(oh.astype(BF16))
    o = jnp.concatenate(outs, axis=-1)
    o_ref[0] = h + _dot(o, wo_ref[...])


def cross_attention(h, g, wq, kv, wo):
    bsz, s, d = h.shape
    tm = min(ROW_TILE, s)
    mem_len = kv.shape[1]
    return pl.pallas_call(
        _xattn_kernel,
        grid=(bsz, s // tm),
        in_specs=[pl.BlockSpec((1, tm, d), lambda b, i: (b, i, 0)),
                  pl.BlockSpec((1, d), lambda b, i: (0, 0)),
                  pl.BlockSpec((d, XA_WIDTH), lambda b, i: (0, 0)),
                  pl.BlockSpec((1, mem_len, 2 * XA_WIDTH), lambda b, i: (b, 0, 0)),
                  pl.BlockSpec((XA_WIDTH, d), lambda b, i: (0, 0))],
        out_specs=pl.BlockSpec((1, tm, d), lambda b, i: (b, i, 0)),
        out_shape=jax.ShapeDtypeStruct((bsz, s, d), F32),
        compiler_params=_cparams(("parallel", "parallel")),
        name="cross_attention",
    )(h, g.reshape(1, d), wq, kv, wo)


def _swiglu_tile(u, wg_ref, wu_ref, wd_ref):
    acc = None
    for c0 in range(0, D_FF, FF_CHUNK):
        gt = _dot(u, wg_ref[0, :, c0:c0 + FF_CHUNK])
        up = _dot(u, wu_ref[0, :, c0:c0 + FF_CHUNK])
        hid = (_silu(gt) * up).astype(BF16)
        t = _dot(hid, wd_ref[0, c0:c0 + FF_CHUNK, :])
        acc = t if acc is None else acc + t
    return acc


def _dense_ffn_kernel(h_ref, g_ref, wg_ref, wu_ref, wd_ref, o_ref):
    h = h_ref[...]
    u = _rms(h, g_ref[...]).astype(BF16)
    o_ref[...] = h + _swiglu_tile(u, wg_ref, wu_ref, wd_ref)


def dense_ffn(h, g, wg, wu, wd, layer):
    m, d = h.shape
    tm = min(ROW_TILE, m)
    return pl.pallas_call(
        _dense_ffn_kernel,
        grid=(m // tm,),
        in_specs=[pl.BlockSpec((tm, d), lambda i: (i, 0)),
                  pl.BlockSpec((1, d), lambda i: (0, 0)),
                  pl.BlockSpec((1, d, D_FF), lambda i: (layer, 0, 0)),
                  pl.BlockSpec((1, d, D_FF), lambda i: (layer, 0, 0)),
                  pl.BlockSpec((1, D_FF, d), lambda i: (layer, 0, 0))],
        out_specs=pl.BlockSpec((tm, d), lambda i: (i, 0)),
        out_shape=jax.ShapeDtypeStruct((m, d), F32),
        compiler_params=_cparams(("parallel",)),
        name="dense_ffn",
    )(h, g.reshape(1, d), wg, wu, wd)


def _expert_ffn_kernel(te_ref, nt_ref, x_ref, wg_ref, wu_ref, wd_ref, o_ref):
    @pl.when(pl.program_id(0) < nt_ref[0])
    def _():
        o_ref[...] = _swiglu_tile(x_ref[...], wg_ref, wu_ref, wd_ref).astype(o_ref.dtype)

    @pl.when(pl.program_id(0) >= nt_ref[0])
    def _():
        o_ref[...] = jnp.zeros_like(o_ref)


def expert_ffn(x_sorted, tile_expert, n_tiles_used, wg, wu, wd, tm):
    r, d = x_sorted.shape
    grid_spec = pltpu.PrefetchScalarGridSpec(
        num_scalar_prefetch=2,
        grid=(r // tm,),
        in_specs=[pl.BlockSpec((tm, d), lambda i, te, nt: (i, 0)),
                  pl.BlockSpec((1, d, D_FF), lambda i, te, nt: (te[i], 0, 0)),
                  pl.BlockSpec((1, d, D_FF), lambda i, te, nt: (te[i], 0, 0)),
                  pl.BlockSpec((1, D_FF, d), lambda i, te, nt: (te[i], 0, 0))],
        out_specs=pl.BlockSpec((tm, d), lambda i, te, nt: (i, 0)),
    )
    return pl.pallas_call(
        _expert_ffn_kernel,
        grid_spec=grid_spec,
        out_shape=jax.ShapeDtypeStruct((r, d), BF16),
        compiler_params=_cparams(("arbitrary",)),
        name="expert_ffn",
    )(tile_expert, n_tiles_used, x_sorted, wg, wu, wd)


def _router_kernel(h_ref, g_ref, wr_hi_ref, wr_lo_ref, earlier_ref, u_ref, route_ref, fields_ref, counts_ref):
    uf = _rms(h_ref[...], g_ref[...])
    u_hi = uf.astype(BF16)
    u_ref[...] = u_hi
    u_lo = (uf - u_hi.astype(F32)).astype(BF16)
    logits = _dot(u_hi, wr_hi_ref[...]) + (_dot(u_lo, wr_hi_ref[...]) + _dot(u_hi, wr_lo_ref[...]))
    lane = lax.broadcasted_iota(jnp.int32, logits.shape, 1)
    logits = jnp.where(lane < N_EXPERTS, logits, -jnp.inf)
    m1 = jnp.max(logits, axis=-1, keepdims=True)
    i1 = jnp.min(jnp.where(logits == m1, lane, LANES), axis=-1, keepdims=True)
    rest = jnp.where(lane == i1, -jnp.inf, logits)
    m2 = jnp.max(rest, axis=-1, keepdims=True)
    i2 = jnp.min(jnp.where(rest == m2, lane, LANES), axis=-1, keepdims=True)
    e2 = jnp.exp(m2 - m1)
    w1 = 1.0 / (1.0 + e2)
    w2 = e2 / (1.0 + e2)
    route = jnp.where(lane == 0, w1, jnp.where(lane == 1, w2, 0.0))
    route = jnp.where(lane == 2, i1.astype(F32), route)
    route = jnp.where(lane == 3, i2.astype(F32), route)
    tm = logits.shape[0]
    picked = jnp.where(lane == i1, 1.0, jnp.where(lane == i2, 1.0, 0.0))
    before = _dot(earlier_ref[...], picked.astype(BF16))
    rank1 = jnp.sum(jnp.where(lane == i1, before, 0.0), axis=-1, keepdims=True)
    rank2 = jnp.sum(jnp.where(lane == i2, before, 0.0), axis=-1, keepdims=True)
    route = jnp.where(lane == 4, rank1, route)
    route = jnp.where(lane == 5, rank2, route)
    route_ref[...] = route
    for b in range(tm // LANES):
        fields_ref[:, b * LANES:(b + 1) * LANES] = route[b * LANES:(b + 1) * LANES, :].T[0:8, :]
    counts_ref[...] = jnp.broadcast_to(jnp.sum(picked, axis=0, keepdims=True), counts_ref.shape)


def router(h, g, w_router):
    m, d = h.shape
    tm = min(ROW_TILE, m)
    nt = m // tm
    wr = jnp.pad(w_router.astype(F32), ((0, 0), (0, LANES - N_EXPERTS)))
    wr_hi = wr.astype(BF16)
    wr_lo = (wr - wr_hi.astype(F32)).astype(BF16)
    earlier = (jnp.arange(tm)[None, :] < jnp.arange(tm)[:, None]).astype(BF16)
    return pl.pallas_call(
        _router_kernel,
        grid=(nt,),
        in_specs=[pl.BlockSpec((tm, d), lambda i: (i, 0)),
                  pl.BlockSpec((1, d), lambda i: (0, 0)),
                  pl.BlockSpec((d, LANES), lambda i: (0, 0)),
                  pl.BlockSpec((d, LANES), lambda i: (0, 0)),
                  pl.BlockSpec((tm, tm), lambda i: (0, 0))],
        out_specs=[pl.BlockSpec((tm, d), lambda i: (i, 0)),
                   pl.BlockSpec((tm, LANES), lambda i: (i, 0)),
                   pl.BlockSpec((8, tm), lambda i: (0, i)),
                   pl.BlockSpec((8, LANES), lambda i: (i, 0))],
        out_shape=[jax.ShapeDtypeStruct((m, d), BF16),
                   jax.ShapeDtypeStruct((m, LANES), F32),
                   jax.ShapeDtypeStruct((8, m), F32),
                   jax.ShapeDtypeStruct((8 * nt, LANES), F32)],
        compiler_params=_cparams(("parallel",)),
        name="router",
    )(h, g.reshape(1, d), wr_hi, wr_lo, earlier)


GRAN = 16
LOC_ROWS = TOP_K * ROW_TILE + N_EXPERTS * GRAN
MAX_GRAN = LOC_ROWS // GRAN


def _dispatch_kernel(gdst_ref, ngran_ref, ls_ref, u_ref, f_ref, xin_ref, xout_ref, xs_scr, sem):
    del xin_ref
    t = pl.program_id(0)
    nt = pl.num_programs(0)
    tm = u_ref.shape[0]
    f = f_ref[...]

    def local_row(expert_row, rank_row):
        base = jnp.zeros_like(rank_row)
        for e in range(N_EXPERTS):
            base = jnp.where(expert_row == float(e), ls_ref[t * N_EXPERTS + e].astype(F32), base)
        return (base + rank_row).astype(jnp.int32)

    lp1 = local_row(f[2:3, :], f[4:5, :])
    lp2 = local_row(f[3:4, :], f[5:6, :])
    r = lax.broadcasted_iota(jnp.int32, (LOC_ROWS, tm), 0)
    sel = jnp.where(r == lp1, 1.0, jnp.where(r == lp2, 1.0, 0.0)).astype(BF16)
    xs = _dot(sel, u_ref[...]).astype(BF16)

    def piece(tile, q):
        src = xs_scr.at[pl.ds(pl.multiple_of(q * GRAN, GRAN), GRAN), :]
        dst = xout_ref.at[pl.ds(pl.multiple_of(gdst_ref[tile * MAX_GRAN + q], GRAN), GRAN), :]
        return pltpu.make_async_copy(src, dst, sem.at[0])

    def wait_all(tile):
        lax.fori_loop(0, ngran_ref[tile], lambda q, c: (piece(tile, q).wait(), c)[1], 0)

    @pl.when(t > 0)
    def _():
        wait_all(t - 1)

    xs_scr[...] = xs
    lax.fori_loop(0, ngran_ref[t], lambda q, c: (piece(t, q).start(), c)[1], 0)

    @pl.when(t == nt - 1)
    def _():
        wait_all(t)


def dispatch(u, fields, gdst, ngran, seg_ls, n_rows):
    m, d = u.shape
    tm = min(ROW_TILE, m)
    grid_spec = pltpu.PrefetchScalarGridSpec(
        num_scalar_prefetch=3,
        grid=(m // tm,),
        in_specs=[pl.BlockSpec((tm, d), lambda i, *_: (i, 0)),
                  pl.BlockSpec((8, tm), lambda i, *_: (0, i)),
                  pl.BlockSpec(memory_space=pl.ANY)],
        out_specs=pl.BlockSpec(memory_space=pl.ANY),
        scratch_shapes=[pltpu.VMEM((LOC_ROWS, d), BF16), pltpu.SemaphoreType.DMA((1,))],
    )
    return pl.pallas_call(
        _dispatch_kernel,
        grid_spec=grid_spec,
        out_shape=jax.ShapeDtypeStruct((n_rows, d), BF16),
        input_output_aliases={5: 0},
        compiler_params=_cparams(("arbitrary",)),
        name="moe_dispatch",
    )(gdst, ngran, seg_ls, u, fields, jnp.zeros((n_rows, d), BF16))


def _combine_kernel(gdst_ref, ngran_ref, h_ref, route_ref, ls_ref, g_ref, y_ref, o_ref, y_scr, sem,
                    *, final_norm):
    t = pl.program_id(0)
    nt = pl.num_programs(0)
    tm = h_ref.shape[0]
    slot = lax.rem(t, 2)

    def piece(tile, q):
        buf = lax.rem(tile, 2)
        src = y_ref.at[pl.ds(pl.multiple_of(gdst_ref[tile * MAX_GRAN + q], GRAN), GRAN), :]
        dst = y_scr.at[buf, pl.ds(pl.multiple_of(q * GRAN, GRAN), GRAN), :]
        return pltpu.make_async_copy(src, dst, sem.at[buf])

    def fetch(tile):
        lax.fori_loop(0, ngran_ref[tile], lambda q, c: (piece(tile, q).start(), c)[1], 0)

    @pl.when(t == 0)
    def _():
        y_scr[...] = jnp.zeros_like(y_scr)
        fetch(t)

    @pl.when(t + 1 < nt)
    def _():
        fetch(t + 1)

    route = route_ref[...]
    lane = lax.broadcasted_iota(jnp.int32, (tm, LANES), 1)
    ls_row = ls_ref[0:1, :]

    def local_row(expert_col, rank_col):
        base = jnp.sum(jnp.where(lane == expert_col.astype(jnp.int32), ls_row, 0.0), axis=-1, keepdims=True)
        return (base + rank_col).astype(jnp.int32)

    c = lax.broadcasted_iota(jnp.int32, (tm, LOC_ROWS), 1)
    sel1 = jnp.where(c == local_row(route[:, 2:3], route[:, 4:5]), 1.0, 0.0).astype(BF16)
    sel2 = jnp.where(c == local_row(route[:, 3:4], route[:, 5:6]), 1.0, 0.0).astype(BF16)

    lax.fori_loop(0, ngran_ref[t], lambda q, c_: (piece(t, q).wait(), c_)[1], 0)
    y = y_scr[slot]
    out = h_ref[...] + (route[:, 0:1] * _dot(sel1, y) + route[:, 1:2] * _dot(sel2, y))
    if final_norm:
        out = _rms(out, g_ref[...])
    o_ref[...] = out


def combine(h, y_sorted, route, ls_rows, gdst, ngran, g, final_norm):
    m, d = h.shape
    tm = min(ROW_TILE, m)
    grid_spec = pltpu.PrefetchScalarGridSpec(
        num_scalar_prefetch=2,
        grid=(m // tm,),
        in_specs=[pl.BlockSpec((tm, d), lambda i, *_: (i, 0)),
                  pl.BlockSpec((tm, LANES), lambda i, *_: (i, 0)),
                  pl.BlockSpec((8, LANES), lambda i, *_: (i, 0)),
                  pl.BlockSpec((1, d), lambda i, *_: (0, 0)),
                  pl.BlockSpec(memory_space=pl.ANY)],
        out_specs=pl.BlockSpec((tm, d), lambda i, *_: (i, 0)),
        scratch_shapes=[pltpu.VMEM((2, LOC_ROWS, d), BF16), pltpu.SemaphoreType.DMA((2,))],
    )
    return pl.pallas_call(
        functools.partial(_combine_kernel, final_norm=final_norm),
        grid_spec=grid_spec,
        out_shape=jax.ShapeDtypeStruct((m, d), F32),
        compiler_params=_cparams(("arbitrary",)),
        name="moe_combine",
    )(gdst, ngran, h, route, ls_rows, g.reshape(1, d), y_sorted)


def _final_norm_kernel(h_ref, g_ref, o_ref):
    o_ref[...] = _rms(h_ref[...], g_ref[...])


def final_norm_rows(h, g):
    m, d = h.shape
    tm = min(ROW_TILE, m)
    return pl.pallas_call(
        _final_norm_kernel,
        grid=(m // tm,),
        in_specs=[pl.BlockSpec((tm, d), lambda i: (i, 0)), pl.BlockSpec((1, d), lambda i: (0, 0))],
        out_specs=pl.BlockSpec((tm, d), lambda i: (i, 0)),
        out_shape=jax.ShapeDtypeStruct((m, d), F32),
        compiler_params=_cparams(("parallel",)),
        name="final_norm",
    )(h, g.reshape(1, d))


def moe_ffn(h, g, w_router, wg, wu, wd, layer, g_final, apply_final_norm):
    m, d = h.shape
    tm = min(MOE_TILE, m)
    nt = m // tm
    u, route, fields, counts = router(h, g, w_router)
    counts = counts.reshape(nt, 8, LANES)[:, 0, :N_EXPERTS].astype(jnp.int32)
    seg = (counts + GRAN - 1) // GRAN * GRAN
    loc_end = jnp.cumsum(seg, axis=1)
    loc_start = loc_end - seg
    group_tiles = (jnp.sum(seg, axis=0) + tm - 1) // tm
    tile_end = jnp.cumsum(group_tiles)
    group_start = (tile_end - group_tiles) * tm
    seg_dst = group_start[None, :] + jnp.cumsum(seg, axis=0) - seg
    q = jnp.arange(MAX_GRAN, dtype=jnp.int32) * GRAN
    piece_expert = jnp.minimum(
        jnp.sum((q[None, :, None] >= loc_end[:, None, :]).astype(jnp.int32), axis=-1), N_EXPERTS - 1)
    owner = piece_expert[:, :, None] == jnp.arange(N_EXPERTS)[None, None, :]
    gdst = (jnp.sum(jnp.where(owner, (seg_dst - loc_start)[:, None, :], 0), axis=-1)
            + q[None, :]).reshape(-1).astype(jnp.int32)
    ngran = (loc_end[:, -1] // GRAN).astype(jnp.int32)
    n_rows = -(-(TOP_K * m + nt * N_EXPERTS * (GRAN - 1) + N_EXPERTS * (tm - 1)) // tm) * tm
    n_tiles = n_rows // tm
    tile_expert = jnp.minimum(
        jnp.sum((jnp.arange(n_tiles)[:, None] >= tile_end[None, :]).astype(jnp.int32), axis=1),
        N_EXPERTS - 1).astype(jnp.int32) + layer * N_EXPERTS
    n_used = tile_end[-1:].astype(jnp.int32)
    ls_rows = jnp.broadcast_to(
        jnp.pad(loc_start.astype(F32), ((0, 0), (0, LANES - N_EXPERTS)))[:, None, :], (nt, 8, LANES)
    ).reshape(nt * 8, LANES)
    x_sorted = dispatch(u, fields, gdst, ngran, loc_start.reshape(-1).astype(jnp.int32), n_rows)
    y = expert_ffn(x_sorted, tile_expert, n_used, wg, wu, wd, tm)
    return combine(h, y, route, ls_rows, gdst, ngran, g_final, apply_final_norm)


def _cast_kernel(x_ref, o_ref):
    o_ref[...] = x_ref[...].astype(o_ref.dtype)


CAST_BLOCK_BYTES = 6 * 1024 * 1024


def cast_bf16(w):
    rows, cols = w.shape[-2:]
    lead = w.size // (rows * cols)
    tr = next(rows // k for k in range(1, rows // 16 + 1)
              if rows % k == 0 and (rows // k) % 16 == 0 and (rows // k) * cols * 4 <= CAST_BLOCK_BYTES)
    out = pl.pallas_call(
        _cast_kernel,
        grid=(lead, rows // tr),
        in_specs=[pl.BlockSpec((1, tr, cols), lambda l, i: (l, i, 0))],
        out_specs=pl.BlockSpec((1, tr, cols), lambda l, i: (l, i, 0)),
        out_shape=jax.ShapeDtypeStruct((lead, rows, cols), BF16),
        compiler_params=_cparams(("parallel", "parallel")),
        name="cast_bf16",
    )(w.reshape(lead, rows, cols))
    return out.reshape(w.shape)


def _in_proj_weight(w):
    o_xbc = SSD_INNER
    o_dt = o_xbc + SSD_CONV_DIM
    o_qkv = o_dt + SSD_HEADS
    o_g = o_qkv + 3 * SB_WIDTH
    dt_cols = jnp.pad(w[:, o_dt:o_qkv], ((0, 0), (0, LANES - SSD_HEADS)))
    return jnp.concatenate([w[:, :o_dt], dt_cols, w[:, o_qkv:]], axis=1)


def kernel(x, mem, norm_mix, w_in, conv_w, conv_b, dt_bias, a_log, d_skip, ssd_norm, w_ssd_o, w_sb_o, w_out, norm_xa, norm_mem, xa_wq, xa_wk, xa_wv, xa_wo, norm_ffn, ffn_w_gate, ffn_w_up, ffn_w_down, moe_router, moe_w_gate, moe_w_up, moe_w_down, final_norm):
    bsz, s, d = x.shape
    m = bsz * s
    depth = w_in.shape[0]
    mem2 = mem.reshape(-1, d)
    h = x.reshape(m, d)
    segs = (SEG_Z, SEG_XBC, SEG_DT, SEG_Q, SEG_K, SEG_V, SEG_GSSD, SEG_GSB)
    seg_dtypes = (BF16, BF16, F32, BF16, BF16, BF16, BF16, BF16)
    w_in, w_ssd_o, w_sb_o, w_out, xa_wq, xa_wk, xa_wv, xa_wo = [
        cast_bf16(w) for w in (w_in, w_ssd_o, w_sb_o, w_out, xa_wq, xa_wk, xa_wv, xa_wo)]
    ffn_w = [cast_bf16(w) for w in (ffn_w_gate, ffn_w_up, ffn_w_down)]
    moe_w = [cast_bf16(w).reshape((-1,) + w.shape[2:]) for w in (moe_w_gate, moe_w_up, moe_w_down)]
    for i in range(depth):
        z, xbc, dt, q, k, v, g_ssd, g_sb = norm_proj(
            h, norm_mix[i], _in_proj_weight(w_in[i]), segs, seg_dtypes, "in_proj")
        sh = lambda a: a.reshape(bsz, s, a.shape[-1])
        y_ssd = ssd_branch(sh(z), sh(xbc), sh(dt), conv_w[i], conv_b[i], dt_bias[i], a_log[i],
                           d_skip[i], ssd_norm[i])
        y_sb = stick_breaking(sh(q), sh(k), sh(v))
        h = merge_mixers(h, y_ssd.reshape(m, -1), y_sb.reshape(m, -1), g_ssd, g_sb,
                         w_ssd_o[i], w_sb_o[i], w_out[i])
        w_kv = jnp.concatenate([xa_wk[i], xa_wv[i]], axis=1)
        (kv,) = norm_proj(mem2, norm_mem[i], w_kv, ((0, 2 * XA_WIDTH),), (BF16,), "mem_kv")
        h = cross_attention(h.reshape(bsz, s, d), norm_xa[i], xa_wq[i],
                            kv.reshape(bsz, -1, 2 * XA_WIDTH), xa_wo[i]).reshape(m, d)
        j = i // 2
        last = i == depth - 1
        if i % 2 == 0:
            h = dense_ffn(h, norm_ffn[i], *ffn_w, j)
            if last:
                h = final_norm_rows(h, final_norm)
        else:
            h = moe_ffn(h, norm_ffn[i], moe_router[j], *moe_w, j, final_norm, last)
    return h.reshape(bsz, s, d)
```

```python
import functools

import jax
import jax.numpy as jnp
from jax import lax
from jax.experimental import pallas as pl
from jax.experimental.pallas import tpu as pltpu

F32 = jnp.float32
BF16 = jnp.bfloat16

D_MODEL = 1024
SSD_HEAD_DIM = 64
SSD_INNER = 1024
SSD_HEADS = 16
SSD_GROUPS = 2
SSD_STATE = 64
SSD_CONV = 4
SSD_CHUNK = 128
SSD_CONV_DIM = SSD_INNER + 2 * SSD_GROUPS * SSD_STATE
SB_HEADS = 8
SB_HEAD_DIM = 64
SB_WIDTH = 512
SB_BLOCK = 128
XA_HEADS = 4
XA_HEAD_DIM = 128
XA_WIDTH = 512
D_FF = 2816
N_EXPERTS = 8
TOP_K = 2
EPS = 1e-6

LANES = 128
VMEM_LIMIT = 56 * 1024 * 1024
ROW_TILE = 512
FF_CHUNK = 256
MOE_TILE = 512

SEG_Z = (0, 1024)
SEG_XBC = (1024, 1280)
SEG_DT = (2304, 128)
SEG_Q = (2432, 512)
SEG_K = (2944, 512)
SEG_V = (3456, 512)
SEG_GSSD = (3968, 1024)
SEG_GSB = (4992, 1024)
IN_PAD_WIDTH = 6016


def _cparams(sem):
    return pltpu.CompilerParams(dimension_semantics=sem, vmem_limit_bytes=VMEM_LIMIT)


def _rms(x, g):
    return x * lax.rsqrt(jnp.mean(x * x, axis=-1, keepdims=True) + EPS) * g


def _split2(x):
    hi = x.astype(BF16)
    lo = (x - hi.astype(F32)).astype(BF16)
    return hi, lo


def _dot(a, b):
    return jnp.dot(a, b, preferred_element_type=F32)


def _dot_nt(a, b):
    return lax.dot_general(a, b, (((1,), (1,)), ((), ())), preferred_element_type=F32)


def _dot_tn(a, b):
    return lax.dot_general(a, b, (((0,), (0,)), ((), ())), preferred_element_type=F32)


def _dot_split(x_f32, m_bf16):
    hi, lo = _split2(x_f32)
    return _dot(hi, m_bf16) + _dot(lo, m_bf16)


def _silu(x):
    return x / (1.0 + jnp.exp(-x))


def _sigmoid(x):
    return 1.0 / (1.0 + jnp.exp(-x))


def _norm_proj_kernel(x_ref, g_ref, w_ref, *out_refs, segs):
    x = x_ref[...]
    u = _rms(x, g_ref[...]).astype(BF16)
    for (start, width), o_ref in zip(segs, out_refs):
        for c0 in range(0, width, 512):
            cw = min(512, width - c0)
            r = _dot(u, w_ref[:, start + c0:start + c0 + cw])
            o_ref[:, c0:c0 + cw] = r.astype(o_ref.dtype)


def norm_proj(x, g, w, segs, dtypes, name):
    m, k = x.shape
    tm = min(ROW_TILE, m)
    n = w.shape[1]
    out_shape = [jax.ShapeDtypeStruct((m, wd), dt) for (_, wd), dt in zip(segs, dtypes)]
    out_specs = [pl.BlockSpec((tm, wd), lambda i: (i, 0)) for (_, wd) in segs]
    return pl.pallas_call(
        functools.partial(_norm_proj_kernel, segs=tuple(segs)),
        grid=(m // tm,),
        in_specs=[pl.BlockSpec((tm, k), lambda i: (i, 0)),
                  pl.BlockSpec((1, k), lambda i: (0, 0)),
                  pl.BlockSpec((k, n), lambda i: (0, 0))],
        out_specs=out_specs,
        out_shape=out_shape,
        compiler_params=_cparams(("parallel",)),
        name=name,
    )(x, g.reshape(1, k), w)


def _ssd_kernel(z_ref, xbc_ref, dt_ref, cw_ref, cb_ref, dtb_ref, aneg_ref, dsk_ref,
                ng_ref, e_ref, o_ref, halo_scr, state_scr, y_scr):
    L = SSD_CHUNK
    c = pl.program_id(1)

    @pl.when(c == 0)
    def _():
        halo_scr[...] = jnp.zeros_like(halo_scr)
        state_scr[...] = jnp.zeros_like(state_scr)

    x_cur = xbc_ref[0].astype(F32)
    halo = halo_scr[...]
    sub = lax.broadcasted_iota(jnp.int32, (8, SSD_CONV_DIM), 0)
    acc = cb_ref[...] + x_cur * cw_ref[SSD_CONV - 1:SSD_CONV, :]
    for d in range(1, SSD_CONV):
        rolled = pltpu.roll(x_cur, d, axis=0)
        head = jnp.where(sub < d, pltpu.roll(halo, d, axis=0), rolled[0:8, :])
        shifted = jnp.concatenate([head, rolled[8:, :]], axis=0)
        acc = acc + shifted * cw_ref[SSD_CONV - 1 - d:SSD_CONV - d, :]
    halo_scr[...] = x_cur[L - 8:L, :]
    xc = _silu(acc)
    xs = xc[:, :SSD_INNER]
    b128 = xc[:, SSD_INNER:SSD_INNER + LANES].astype(BF16)
    c128 = xc[:, SSD_INNER + LANES:SSD_INNER + 2 * LANES]

    row = lax.broadcasted_iota(jnp.int32, (L, L), 0)
    col = lax.broadcasted_iota(jnp.int32, (L, L), 1)
    causal = col <= row
    tril = jnp.where(causal, 1.0, 0.0).astype(BF16)

    dt_in = dt_ref[0] + dtb_ref[...]
    dt = jnp.maximum(dt_in, 0.0) + jnp.log(1.0 + jnp.exp(-jnp.abs(dt_in)))
    a = dt * aneg_ref[...]
    a_hi = a.astype(BF16)
    a_mid = (a - a_hi.astype(F32))
    a_mid_b = a_mid.astype(BF16)
    a_lo = (a_mid - a_mid_b.astype(F32)).astype(BF16)
    acum = _dot(tril, a_hi) + _dot(tril, a_mid_b) + _dot(tril, a_lo)
    acum_t = acum.T
    alast = acum[L - 1:L, :]

    expand = e_ref[...]
    ea_e = _dot_split(jnp.exp(acum), expand)
    w_e = _dot_split(dt * jnp.exp(alast - acum), expand)
    cd_e = _dot_split(jnp.broadcast_to(jnp.exp(alast), (8, LANES)), expand)[0:1, :]
    dt_t = dt.T

    x_b = xs.astype(BF16)
    xw_b = (xs * w_e).astype(BF16)

    lane_half = lax.broadcasted_iota(jnp.int32, (L, LANES), 1) < SSD_STATE
    zero_b = jnp.zeros((L, LANES), BF16)
    c128_b = c128.astype(BF16)
    cb_g = [
        _dot_nt(jnp.where(lane_half, c128_b, zero_b), b128),
        _dot_nt(jnp.where(lane_half, zero_b, c128_b), b128),
    ]
    heads_per_group = SSD_HEADS // SSD_GROUPS
    for p in range(SSD_HEADS // 2):
        xp = x_b[:, p * LANES:(p + 1) * LANES]
        y_pair = None
        for s in range(2):
            h = 2 * p + s
            seg = acum[:, h:h + 1] - acum_t[h:h + 1, :]
            decay = jnp.exp(jnp.where(causal, seg, -jnp.inf))
            m_h = (cb_g[h // heads_per_group] * decay * dt_t[h:h + 1, :]).astype(BF16)
            x_h = jnp.where(lane_half, xp, zero_b) if s == 0 else jnp.where(lane_half, zero_b, xp)
            t = _dot(m_h, x_h)
            y_pair = t if y_pair is None else y_pair + t
        y_scr[:, p * LANES:(p + 1) * LANES] = y_pair

    state = state_scr[...]
    y_off = _dot(c128_b, state.astype(BF16)) * ea_e
    s_new = _dot_tn(b128, xw_b)
    srow = lax.broadcasted_iota(jnp.int32, (LANES, SSD_INNER), 0) < SSD_STATE
    scol = lax.broadcasted_iota(jnp.int32, (LANES, SSD_INNER), 1) < SSD_INNER // SSD_GROUPS
    state_scr[...] = state * cd_e + jnp.where(srow == scol, s_new, 0.0)

    y = y_scr[...] + y_off + xs * dsk_ref[...]
    y = y * _silu(z_ref[0].astype(F32))
    gw = SSD_INNER // SSD_GROUPS
    for g in range(SSD_GROUPS):
        yg = y[:, g * gw:(g + 1) * gw]
        o_ref[0, :, g * gw:(g + 1) * gw] = _rms(yg, ng_ref[:, g * gw:(g + 1) * gw]).astype(o_ref.dtype)


def ssd_branch(z, xbc, dt, conv_w, conv_b, dt_bias, a_log, d_skip, norm_g):
    bsz, s, _ = z.shape
    nc = s // SSD_CHUNK
    pad = LANES - SSD_HEADS
    dtb = jnp.pad(dt_bias.astype(F32), (0, pad)).reshape(1, LANES)
    aneg = jnp.pad(-jnp.exp(a_log.astype(F32)), (0, pad)).reshape(1, LANES)
    dsk = jnp.repeat(d_skip.astype(F32), SSD_HEAD_DIM).reshape(1, SSD_INNER)
    expand = (jnp.arange(LANES)[:, None] == (jnp.arange(SSD_INNER)[None, :] // SSD_HEAD_DIM)).astype(BF16)
    const = lambda shape: pl.BlockSpec(shape, lambda b, c: (0,) * len(shape))
    return pl.pallas_call(
        _ssd_kernel,
        grid=(bsz, nc),
        in_specs=[pl.BlockSpec((1, SSD_CHUNK, SSD_INNER), lambda b, c: (b, c, 0)),
                  pl.BlockSpec((1, SSD_CHUNK, SSD_CONV_DIM), lambda b, c: (b, c, 0)),
                  pl.BlockSpec((1, SSD_CHUNK, LANES), lambda b, c: (b, c, 0)),
                  const((SSD_CONV, SSD_CONV_DIM)),
                  const((1, SSD_CONV_DIM)),
                  const((1, LANES)),
                  const((1, LANES)),
                  const((1, SSD_INNER)),
                  const((1, SSD_INNER)),
                  const((LANES, SSD_INNER))],
        out_specs=pl.BlockSpec((1, SSD_CHUNK, SSD_INNER), lambda b, c: (b, c, 0)),
        out_shape=jax.ShapeDtypeStruct((bsz, s, SSD_INNER), BF16),
        scratch_shapes=[pltpu.VMEM((8, SSD_CONV_DIM), F32),
                        pltpu.VMEM((LANES, SSD_INNER), F32),
                        pltpu.VMEM((SSD_CHUNK, SSD_INNER), F32)],
        compiler_params=_cparams(("parallel", "arbitrary")),
        name="ssd_scan",
    )(z, xbc, dt, conv_w.astype(F32), conv_b.astype(F32).reshape(1, -1), dtb, aneg, dsk,
      norm_g.astype(F32).reshape(1, -1), expand)


SB_PAIRS = SB_WIDTH // LANES
EXP_UNDERFLOW = -104.0


def _sb_kernel(q_ref, k_ref, v_ref, nsuffix_ref, o_ref, run_scr, acc_scr, z_scr):
    T = SB_BLOCK
    qi = pl.program_id(1)
    nsuffix = nsuffix_ref[...]
    rw = lax.broadcasted_iota(jnp.int32, (T, 2 * T), 0)
    cw = lax.broadcasted_iota(jnp.int32, (T, 2 * T), 1)
    strict_w = (cw & (T - 1)) < rw
    strict = strict_w[:, :T]
    q_all = q_ref[0] * jnp.asarray(SB_HEAD_DIM ** -0.5, BF16)
    pairs = range(SB_PAIRS)
    lane_lo = lax.broadcasted_iota(jnp.int32, (T, LANES), 1) < SB_HEAD_DIM
    zero_b = jnp.zeros((T, LANES), BF16)

    run_scr[...] = jnp.zeros_like(run_scr)
    acc_scr[...] = jnp.zeros_like(acc_scr)

    def pair_rows(ref, j, p):
        x = ref[0, pl.ds(pl.multiple_of(j * T, T), T), p * LANES:(p + 1) * LANES]
        return jnp.concatenate([jnp.where(lane_lo, x, zero_b), jnp.where(lane_lo, zero_b, x)], axis=0)

    def scores(j):
        for p in pairs:
            z_scr[p] = _dot_nt(q_all[:, p * LANES:(p + 1) * LANES], pair_rows(k_ref, j, p))

    scores(qi)

    def block(j, diag):
        runmax = None
        sps, lsigs = [], []
        for p in pairs:
            z = z_scr[p]
            sp = jnp.maximum(z, 0.0) + jnp.log(1.0 + jnp.exp(-jnp.abs(z)))
            lsigs.append(z - sp)
            sps.append(jnp.where(strict_w, sp, 0.0) if diag else sp)
        lbs = [_dot(sps[p].astype(BF16), nsuffix) for p in pairs]
        scores(jnp.maximum(j - 1, 0))
        for p in pairs:
            args = lsigs[p] + lbs[p]
            atts = []
            for h in range(2):
                run = run_scr[2 * p + h]
                att = jnp.exp(args[:, h * T:(h + 1) * T] + run)
                if diag:
                    att = jnp.where(strict, att, 0.0)
                atts.append(att.astype(BF16))
                run_new = run[:, 0:1] - jnp.sum(sps[p][:, h * T:(h + 1) * T], axis=-1, keepdims=True)
                run_scr[2 * p + h] = jnp.broadcast_to(run_new, (T, LANES))
                runmax = run_new if runmax is None else jnp.maximum(runmax, run_new)
            acc_scr[p] += _dot(jnp.concatenate(atts, axis=1), pair_rows(v_ref, j, p))
        return (jnp.max(runmax) > EXP_UNDERFLOW).astype(jnp.int32)

    go = block(qi, True)

    def body(carry):
        i, _ = carry
        return i + 1, block(qi - i, False)

    lax.while_loop(lambda c: (c[0] <= qi) & (c[1] > 0), body, (jnp.int32(1), go))
    for p in range(SB_PAIRS):
        o_ref[0, :, p * LANES:(p + 1) * LANES] = acc_scr[p].astype(o_ref.dtype)


def stick_breaking(q, k, v):
    bsz, s, _ = q.shape
    nq = s // SB_BLOCK
    idx = jnp.arange(2 * SB_BLOCK)
    same_head = (idx[:, None] < SB_BLOCK) == (idx[None, :] < SB_BLOCK)
    nsuffix = jnp.where(same_head & (idx[:, None] > idx[None, :]), -1.0, 0.0).astype(BF16)
    return pl.pallas_call(
        _sb_kernel,
        grid=(bsz, nq),
        in_specs=[pl.BlockSpec((1, SB_BLOCK, SB_WIDTH), lambda b, i: (b, i, 0)),
                  pl.BlockSpec((1, s, SB_WIDTH), lambda b, i: (b, 0, 0)),
                  pl.BlockSpec((1, s, SB_WIDTH), lambda b, i: (b, 0, 0)),
                  pl.BlockSpec((2 * SB_BLOCK, 2 * SB_BLOCK), lambda b, i: (0, 0))],
        out_specs=pl.BlockSpec((1, SB_BLOCK, SB_WIDTH), lambda b, i: (b, i, 0)),
        out_shape=jax.ShapeDtypeStruct((bsz, s, SB_WIDTH), BF16),
        scratch_shapes=[pltpu.VMEM((SB_HEADS, SB_BLOCK, LANES), F32),
                        pltpu.VMEM((SB_PAIRS, SB_BLOCK, LANES), F32),
                        pltpu.VMEM((SB_PAIRS, SB_BLOCK, 2 * SB_BLOCK), F32)],
        compiler_params=_cparams(("parallel", "arbitrary")),
        name="stick_breaking",
    )(q, k, v, nsuffix)


def _merge_kernel(h_ref, ys_ref, yb_ref, gs_ref, gb_ref, ws_ref, wb_ref, wo_ref, o_ref):
    ms = _sigmoid(gs_ref[...].astype(F32)) * _dot(ys_ref[...], ws_ref[...])
    mb = _sigmoid(gb_ref[...].astype(F32)) * _dot(yb_ref[...], wb_ref[...])
    merged = (ms + mb).astype(BF16)
    o_ref[...] = h_ref[...] + _dot(merged, wo_ref[...])


def merge_mixers(h, y_ssd, y_sb, g_ssd, g_sb, w_ssd_o, w_sb_o, w_out):
    m, d = h.shape
    tm = min(ROW_TILE, m)
    rowspec = lambda w: pl.BlockSpec((tm, w), lambda i: (i, 0))
    wspec = lambda r, c: pl.BlockSpec((r, c), lambda i: (0, 0))
    return pl.pallas_call(
        _merge_kernel,
        grid=(m // tm,),
        in_specs=[rowspec(d), rowspec(SSD_INNER), rowspec(SB_WIDTH), rowspec(d), rowspec(d),
                  wspec(SSD_INNER, d), wspec(SB_WIDTH, d), wspec(d, d)],
        out_specs=rowspec(d),
        out_shape=jax.ShapeDtypeStruct((m, d), F32),
        compiler_params=_cparams(("parallel",)),
        name="merge_mixers",
    )(h, y_ssd, y_sb, g_ssd, g_sb, w_ssd_o, w_sb_o, w_out)


def _xattn_kernel(h_ref, g_ref, wq_ref, kv_ref, wo_ref, o_ref):
    h = h_ref[0]
    u = _rms(h, g_ref[...]).astype(BF16)
    q = _dot(u, wq_ref[...]).astype(BF16)
    scale = XA_HEAD_DIM ** -0.5
    outs = []
    for hd in range(XA_HEADS):
        lo = hd * XA_HEAD_DIM
        kh = kv_ref[0, :, lo:lo + XA_HEAD_DIM]
        vh = kv_ref[0, :, XA_WIDTH + lo:XA_WIDTH + lo + XA_HEAD_DIM]
        s = _dot_nt(q[:, lo:lo + XA_HEAD_DIM], kh) * scale
        e = jnp.exp(s - jnp.max(s, axis=-1, keepdims=True))
        oh = _dot(e.astype(BF16), vh) / jnp.sum(e, axis=-1, keepdims=True)
        outs.append(oh.astype(BF16))
    o = jnp.concatenate(outs, axis=-1)
    o_ref[0] = h + _dot(o, wo_ref[...])


def cross_attention(h, g, wq, kv, wo):
    bsz, s, d = h.shape
    tm = min(ROW_TILE, s)
    mem_len = kv.shape[1]
    return pl.pallas_call(
        _xattn_kernel,
        grid=(bsz, s // tm),
        in_specs=[pl.BlockSpec((1, tm, d), lambda b, i: (b, i, 0)),
                  pl.BlockSpec((1, d), lambda b, i: (0, 0)),
                  pl.BlockSpec((d, XA_WIDTH), lambda b, i: (0, 0)),
                  pl.BlockSpec((1, mem_len, 2 * XA_WIDTH), lambda b, i: (b, 0, 0)),
                  pl.BlockSpec((XA_WIDTH, d), lambda b, i: (0, 0))],
        out_specs=pl.BlockSpec((1, tm, d), lambda b, i: (b, i, 0)),
        out_shape=jax.ShapeDtypeStruct((bsz, s, d), F32),
        compiler_params=_cparams(("parallel", "parallel")),
        name="cross_attention",
    )(h, g.reshape(1, d), wq, kv, wo)


def _swiglu_tile(u, wg_ref, wu_ref, wd_ref):
    acc = None
    for c0 in range(0, D_FF, FF_CHUNK):
        gt = _dot(u, wg_ref[0, :, c0:c0 + FF_CHUNK])
        up = _dot(u, wu_ref[0, :, c0:c0 + FF_CHUNK])
        hid = (_silu(gt) * up).astype(BF16)
        t = _dot(hid, wd_ref[0, c0:c0 + FF_CHUNK, :])
        acc = t if acc is None else acc + t
    return acc


def _dense_ffn_kernel(h_ref, g_ref, wg_ref, wu_ref, wd_ref, o_ref):
    h = h_ref[...]
    u = _rms(h, g_ref[...]).astype(BF16)
    o_ref[...] = h + _swiglu_tile(u, wg_ref, wu_ref, wd_ref)


def dense_ffn(h, g, wg, wu, wd, layer):
    m, d = h.shape
    tm = min(ROW_TILE, m)
    return pl.pallas_call(
        _dense_ffn_kernel,
        grid=(m // tm,),
        in_specs=[pl.BlockSpec((tm, d), lambda i: (i, 0)),
                  pl.BlockSpec((1, d), lambda i: (0, 0)),
                  pl.BlockSpec((1, d, D_FF), lambda i: (layer, 0, 0)),
                  pl.BlockSpec((1, d, D_FF), lambda i: (layer, 0, 0)),
                  pl.BlockSpec((1, D_FF, d), lambda i: (layer, 0, 0))],
        out_specs=pl.BlockSpec((tm, d), lambda i: (i, 0)),
        out_shape=jax.ShapeDtypeStruct((m, d), F32),
        compiler_params=_cparams(("parallel",)),
        name="dense_ffn",
    )(h, g.reshape(1, d), wg, wu, wd)


def _expert_ffn_kernel(te_ref, nt_ref, x_ref, wg_ref, wu_ref, wd_ref, o_ref):
    @pl.when(pl.program_id(0) < nt_ref[0])
    def _():
        o_ref[...] = _swiglu_tile(x_ref[...], wg_ref, wu_ref, wd_ref).astype(o_ref.dtype)

    @pl.when(pl.program_id(0) >= nt_ref[0])
    def _():
        o_ref[...] = jnp.zeros_like(o_ref)


def expert_ffn(x_sorted, tile_expert, n_tiles_used, wg, wu, wd, tm):
    r, d = x_sorted.shape
    grid_spec = pltpu.PrefetchScalarGridSpec(
        num_scalar_prefetch=2,
        grid=(r // tm,),
        in_specs=[pl.BlockSpec((tm, d), lambda i, te, nt: (i, 0)),
                  pl.BlockSpec((1, d, D_FF), lambda i, te, nt: (te[i], 0, 0)),
                  pl.BlockSpec((1, d, D_FF), lambda i, te, nt: (te[i], 0, 0)),
                  pl.BlockSpec((1, D_FF, d), lambda i, te, nt: (te[i], 0, 0))],
        out_specs=pl.BlockSpec((tm, d), lambda i, te, nt: (i, 0)),
    )
    return pl.pallas_call(
        _expert_ffn_kernel,
        grid_spec=grid_spec,
        out_shape=jax.ShapeDtypeStruct((r, d), BF16),
        compiler_params=_cparams(("arbitrary",)),
        name="expert_ffn",
    )(tile_expert, n_tiles_used, x_sorted, wg, wu, wd)


def _router_kernel(h_ref, g_ref, wr_hi_ref, wr_lo_ref, earlier_ref, u_ref, route_ref, fields_ref, counts_ref):
    uf = _rms(h_ref[...], g_ref[...])
    u_hi = uf.astype(BF16)
    u_ref[...] = u_hi
    u_lo = (uf - u_hi.astype(F32)).astype(BF16)
    logits = _dot(u_hi, wr_hi_ref[...]) + (_dot(u_lo, wr_hi_ref[...]) + _dot(u_hi, wr_lo_ref[...]))
    lane = lax.broadcasted_iota(jnp.int32, logits.shape, 1)
    logits = jnp.where(lane < N_EXPERTS, logits, -jnp.inf)
    m1 = jnp.max(logits, axis=-1, keepdims=True)
    i1 = jnp.min(jnp.where(logits == m1, lane, LANES), axis=-1, keepdims=True)
    rest = jnp.where(lane == i1, -jnp.inf, logits)
    m2 = jnp.max(rest, axis=-1, keepdims=True)
    i2 = jnp.min(jnp.where(rest == m2, lane, LANES), axis=-1, keepdims=True)
    e2 = jnp.exp(m2 - m1)
    w1 = 1.0 / (1.0 + e2)
    w2 = e2 / (1.0 + e2)
    route = jnp.where(lane == 0, w1, jnp.where(lane == 1, w2, 0.0))
    route = jnp.where(lane == 2, i1.astype(F32), route)
    route = jnp.where(lane == 3, i2.astype(F32), route)
    tm = logits.shape[0]
    picked = jnp.where(lane == i1, 1.0, jnp.where(lane == i2, 1.0, 0.0))
    before = _dot(earlier_ref[...], picked.astype(BF16))
    rank1 = jnp.sum(jnp.where(lane == i1, before, 0.0), axis=-1, keepdims=True)
    rank2 = jnp.sum(jnp.where(lane == i2, before, 0.0), axis=-1, keepdims=True)
    route = jnp.where(lane == 4, rank1, route)
    route = jnp.where(lane == 5, rank2, route)
    route_ref[...] = route
    for b in range(tm // LANES):
        fields_ref[:, b * LANES:(b + 1) * LANES] = route[b * LANES:(b + 1) * LANES, :].T[0:8, :]
    counts_ref[...] = jnp.broadcast_to(jnp.sum(picked, axis=0, keepdims=True), counts_ref.shape)


def router(h, g, w_router):
    m, d = h.shape
    tm = min(ROW_TILE, m)
    nt = m // tm
    wr = jnp.pad(w_router.astype(F32), ((0, 0), (0, LANES - N_EXPERTS)))
    wr_hi = wr.astype(BF16)
    wr_lo = (wr - wr_hi.astype(F32)).astype(BF16)
    earlier = (jnp.arange(tm)[None, :] < jnp.arange(tm)[:, None]).astype(BF16)
    return pl.pallas_call(
        _router_kernel,
        grid=(nt,),
        in_specs=[pl.BlockSpec((tm, d), lambda i: (i, 0)),
                  pl.BlockSpec((1, d), lambda i: (0, 0)),
                  pl.BlockSpec((d, LANES), lambda i: (0, 0)),
                  pl.BlockSpec((d, LANES), lambda i: (0, 0)),
                  pl.BlockSpec((tm, tm), lambda i: (0, 0))],
        out_specs=[pl.BlockSpec((tm, d), lambda i: (i, 0)),
                   pl.BlockSpec((tm, LANES), lambda i: (i, 0)),
                   pl.BlockSpec((8, tm), lambda i: (0, i)),
                   pl.BlockSpec((8, LANES), lambda i: (i, 0))],
        out_shape=[jax.ShapeDtypeStruct((m, d), BF16),
                   jax.ShapeDtypeStruct((m, LANES), F32),
                   jax.ShapeDtypeStruct((8, m), F32),
                   jax.ShapeDtypeStruct((8 * nt, LANES), F32)],
        compiler_params=_cparams(("parallel",)),
        name="router",
    )(h, g.reshape(1, d), wr_hi, wr_lo, earlier)


GRAN = 16
LOC_ROWS = TOP_K * ROW_TILE + N_EXPERTS * GRAN
MAX_GRAN = LOC_ROWS // GRAN


def _dispatch_kernel(gdst_ref, ngran_ref, ls_ref, zdst_ref, nused_ref, u_ref, f_ref, xout_ref, xs_scr, zero_scr,
                     sem):
    t = pl.program_id(0)
    nt = pl.num_programs(0)
    tm = u_ref.shape[0]
    f = f_ref[...]

    def zero_piece(k):
        dst = xout_ref.at[pl.ds(pl.multiple_of(jnp.maximum(zdst_ref[k], 0), GRAN), GRAN), :]
        return pltpu.make_async_copy(zero_scr.at[0:GRAN, :], dst, sem.at[1])

    def zero_tile(i):
        dst = xout_ref.at[pl.ds(pl.multiple_of(i * tm, tm), tm), :]
        return pltpu.make_async_copy(zero_scr, dst, sem.at[1])

    n_row_tiles = xout_ref.shape[0] // tm

    def for_zero_pieces(fn):
        def body(k, c):
            @pl.when(zdst_ref[k] >= 0)
            def _():
                fn(zero_piece(k))
            return c
        lax.fori_loop(0, zdst_ref.shape[0], body, 0)

    @pl.when(t == 0)
    def _():
        zero_scr[...] = jnp.zeros_like(zero_scr)
        for_zero_pieces(lambda cp: cp.start())
        lax.fori_loop(nused_ref[0], n_row_tiles, lambda i, c: (zero_tile(i).start(), c)[1], 0)

    def local_row(expert_row, rank_row):
        base = jnp.zeros_like(rank_row)
        for e in range(N_EXPERTS):
            base = jnp.where(expert_row == float(e), ls_ref[t * N_EXPERTS + e].astype(F32), base)
        return (base + rank_row).astype(jnp.int32)

    lp1 = local_row(f[2:3, :], f[4:5, :])
    lp2 = local_row(f[3:4, :], f[5:6, :])
    r = lax.broadcasted_iota(jnp.int32, (LOC_ROWS, tm), 0)
    sel = jnp.where(r == lp1, 1.0, jnp.where(r == lp2, 1.0, 0.0)).astype(BF16)
    xs = _dot(sel, u_ref[...]).astype(BF16)

    def piece(tile, q):
        src = xs_scr.at[pl.ds(pl.multiple_of(q * GRAN, GRAN), GRAN), :]
        dst = xout_ref.at[pl.ds(pl.multiple_of(gdst_ref[tile * MAX_GRAN + q], GRAN), GRAN), :]
        return pltpu.make_async_copy(src, dst, sem.at[0])

    def wait_all(tile):
        lax.fori_loop(0, ngran_ref[tile], lambda q, c: (piece(tile, q).wait(), c)[1], 0)

    @pl.when(t > 0)
    def _():
        wait_all(t - 1)

    xs_scr[...] = xs
    lax.fori_loop(0, ngran_ref[t], lambda q, c: (piece(t, q).start(), c)[1], 0)

    @pl.when(t == nt - 1)
    def _():
        wait_all(t)
        for_zero_pieces(lambda cp: cp.wait())
        lax.fori_loop(nused_ref[0], n_row_tiles, lambda i, c: (zero_tile(i).wait(), c)[1], 0)


def dispatch(u, fields, gdst, ngran, seg_ls, zdst, n_used, n_rows):
    m, d = u.shape
    tm = min(ROW_TILE, m)
    grid_spec = pltpu.PrefetchScalarGridSpec(
        num_scalar_prefetch=5,
        grid=(m // tm,),
        in_specs=[pl.BlockSpec((tm, d), lambda i, *_: (i, 0)),
                  pl.BlockSpec((8, tm), lambda i, *_: (0, i))],
        out_specs=pl.BlockSpec(memory_space=pl.ANY),
        scratch_shapes=[pltpu.VMEM((LOC_ROWS, d), BF16), pltpu.VMEM((tm, d), BF16),
                        pltpu.SemaphoreType.DMA((2,))],
    )
    return pl.pallas_call(
        _dispatch_kernel,
        grid_spec=grid_spec,
        out_shape=jax.ShapeDtypeStruct((n_rows, d), BF16),
        compiler_params=_cparams(("arbitrary",)),
        name="moe_dispatch",
    )(gdst, ngran, seg_ls, zdst, n_used, u, fields)


def _combine_kernel(gdst_ref, ngran_ref, h_ref, route_ref, ls_ref, g_ref, y_ref, o_ref, y_scr, sem,
                    *, final_norm):
    t = pl.program_id(0)
    nt = pl.num_programs(0)
    tm = h_ref.shape[0]
    slot = lax.rem(t, 2)

    def piece(tile, q):
        buf = lax.rem(tile, 2)
        src = y_ref.at[pl.ds(pl.multiple_of(gdst_ref[tile * MAX_GRAN + q], GRAN), GRAN), :]
        dst = y_scr.at[buf, pl.ds(pl.multiple_of(q * GRAN, GRAN), GRAN), :]
        return pltpu.make_async_copy(src, dst, sem.at[buf])

    def fetch(tile):
        lax.fori_loop(0, ngran_ref[tile], lambda q, c: (piece(tile, q).start(), c)[1], 0)

    @pl.when(t == 0)
    def _():
        y_scr[...] = jnp.zeros_like(y_scr)
        fetch(t)

    @pl.when(t + 1 < nt)
    def _():
        fetch(t + 1)

    route = route_ref[...]
    lane = lax.broadcasted_iota(jnp.int32, (tm, LANES), 1)
    ls_row = ls_ref[0:1, :]

    def local_row(expert_col, rank_col):
        base = jnp.sum(jnp.where(lane == expert_col.astype(jnp.int32), ls_row, 0.0), axis=-1, keepdims=True)
        return (base + rank_col).astype(jnp.int32)

    c = lax.broadcasted_iota(jnp.int32, (tm, LOC_ROWS), 1)
    sel1 = jnp.where(c == local_row(route[:, 2:3], route[:, 4:5]), 1.0, 0.0).astype(BF16)
    sel2 = jnp.where(c == local_row(route[:, 3:4], route[:, 5:6]), 1.0, 0.0).astype(BF16)

    lax.fori_loop(0, ngran_ref[t], lambda q, c_: (piece(t, q).wait(), c_)[1], 0)
    y = y_scr[slot]
    out = h_ref[...] + (route[:, 0:1] * _dot(sel1, y) + route[:, 1:2] * _dot(sel2, y))
    if final_norm:
        out = _rms(out, g_ref[...])
    o_ref[...] = out


def combine(h, y_sorted, route, ls_rows, gdst, ngran, g, final_norm):
    m, d = h.shape
    tm = min(ROW_TILE, m)
    grid_spec = pltpu.PrefetchScalarGridSpec(
        num_scalar_prefetch=2,
        grid=(m // tm,),
        in_specs=[pl.BlockSpec((tm, d), lambda i, *_: (i, 0)),
                  pl.BlockSpec((tm, LANES), lambda i, *_: (i, 0)),
                  pl.BlockSpec((8, LANES), lambda i, *_: (i, 0)),
                  pl.BlockSpec((1, d), lambda i, *_: (0, 0)),
                  pl.BlockSpec(memory_space=pl.ANY)],
        out_specs=pl.BlockSpec((tm, d), lambda i, *_: (i, 0)),
        scratch_shapes=[pltpu.VMEM((2, LOC_ROWS, d), BF16), pltpu.SemaphoreType.DMA((2,))],
    )
    return pl.pallas_call(
        functools.partial(_combine_kernel, final_norm=final_norm),
        grid_spec=grid_spec,
        out_shape=jax.ShapeDtypeStruct((m, d), F32),
        compiler_params=_cparams(("arbitrary",)),
        name="moe_combine",
    )(gdst, ngran, h, route, ls_rows, g.reshape(1, d), y_sorted)


def _final_norm_kernel(h_ref, g_ref, o_ref):
    o_ref[...] = _rms(h_ref[...], g_ref[...])


def final_norm_rows(h, g):
    m, d = h.shape
    tm = min(ROW_TILE, m)
    return pl.pallas_call(
        _final_norm_kernel,
        grid=(m // tm,),
        in_specs=[pl.BlockSpec((tm, d), lambda i: (i, 0)), pl.BlockSpec((1, d), lambda i: (0, 0))],
        out_specs=pl.BlockSpec((tm, d), lambda i: (i, 0)),
        out_shape=jax.ShapeDtypeStruct((m, d), F32),
        compiler_params=_cparams(("parallel",)),
        name="final_norm",
    )(h, g.reshape(1, d))


def moe_ffn(h, g, w_router, wg, wu, wd, layer, g_final, apply_final_norm):
    m, d = h.shape
    tm = min(MOE_TILE, m)
    nt = m // tm
    u, route, fields, counts = router(h, g, w_router)
    counts = counts.reshape(nt, 8, LANES)[:, 0, :N_EXPERTS].astype(jnp.int32)
    seg = (counts + GRAN - 1) // GRAN * GRAN
    loc_end = jnp.cumsum(seg, axis=1)
    loc_start = loc_end - seg
    group_tiles = (jnp.sum(seg, axis=0) + tm - 1) // tm
    tile_end = jnp.cumsum(group_tiles)
    group_start = (tile_end - group_tiles) * tm
    seg_dst = group_start[None, :] + jnp.cumsum(seg, axis=0) - seg
    q = jnp.arange(MAX_GRAN, dtype=jnp.int32) * GRAN
    piece_expert = jnp.minimum(
        jnp.sum((q[None, :, None] >= loc_end[:, None, :]).astype(jnp.int32), axis=-1), N_EXPERTS - 1)
    owner = piece_expert[:, :, None] == jnp.arange(N_EXPERTS)[None, None, :]
    gdst = (jnp.sum(jnp.where(owner, (seg_dst - loc_start)[:, None, :], 0), axis=-1)
            + q[None, :]).reshape(-1).astype(jnp.int32)
    ngran = (loc_end[:, -1] // GRAN).astype(jnp.int32)
    n_rows = -(-(TOP_K * m + nt * N_EXPERTS * (GRAN - 1) + N_EXPERTS * (tm - 1)) // tm) * tm
    n_tiles = n_rows // tm
    tile_expert = jnp.minimum(
        jnp.sum((jnp.arange(n_tiles)[:, None] >= tile_end[None, :]).astype(jnp.int32), axis=1),
        N_EXPERTS - 1).astype(jnp.int32) + layer * N_EXPERTS
    n_used = tile_end[-1:].astype(jnp.int32)
    ls_rows = jnp.broadcast_to(
        jnp.pad(loc_start.astype(F32), ((0, 0), (0, LANES - N_EXPERTS)))[:, None, :], (nt, 8, LANES)
    ).reshape(nt * 8, LANES)
    group_rows = jnp.sum(seg, axis=0)
    zk = jnp.arange(tm // GRAN - 1, dtype=jnp.int32)[None, :] * GRAN
    zdst = jnp.where(zk < (group_tiles * tm - group_rows)[:, None],
                     (group_start + group_rows)[:, None] + zk, -1).reshape(-1).astype(jnp.int32)
    x_sorted = dispatch(u, fields, gdst, ngran, loc_start.reshape(-1).astype(jnp.int32), zdst, n_used, n_rows)
    y = expert_ffn(x_sorted, tile_expert, n_used, wg, wu, wd, tm)
    return combine(h, y, route, ls_rows, gdst, ngran, g_final, apply_final_norm)


def _cast_kernel(x_ref, o_ref):
    o_ref[...] = x_ref[...].astype(o_ref.dtype)


CAST_BLOCK_BYTES = 6 * 1024 * 1024


def cast_bf16(w):
    rows, cols = w.shape[-2:]
    lead = w.size // (rows * cols)
    tr = next(rows // k for k in range(1, rows // 16 + 1)
              if rows % k == 0 and (rows // k) % 16 == 0 and (rows // k) * cols * 4 <= CAST_BLOCK_BYTES)
    out = pl.pallas_call(
        _cast_kernel,
        grid=(lead, rows // tr),
        in_specs=[pl.BlockSpec((1, tr, cols), lambda l, i: (l, i, 0))],
        out_specs=pl.BlockSpec((1, tr, cols), lambda l, i: (l, i, 0)),
        out_shape=jax.ShapeDtypeStruct((lead, rows, cols), BF16),
        compiler_params=_cparams(("parallel", "parallel")),
        name="cast_bf16",
    )(w.reshape(lead, rows, cols))
    return out.reshape(w.shape)


def _in_proj_weight(w):
    o_xbc = SSD_INNER
    o_dt = o_xbc + SSD_CONV_DIM
    o_qkv = o_dt + SSD_HEADS
    o_g = o_qkv + 3 * SB_WIDTH
    dt_cols = jnp.pad(w[:, o_dt:o_qkv], ((0, 0), (0, LANES - SSD_HEADS)))
    return jnp.concatenate([w[:, :o_dt], dt_cols, w[:, o_qkv:]], axis=1)


def kernel(x, mem, norm_mix, w_in, conv_w, conv_b, dt_bias, a_log, d_skip, ssd_norm, w_ssd_o, w_sb_o, w_out, norm_xa, norm_mem, xa_wq, xa_wk, xa_wv, xa_wo, norm_ffn, ffn_w_gate, ffn_w_up, ffn_w_down, moe_router, moe_w_gate, moe_w_up, moe_w_down, final_norm):
    bsz, s, d = x.shape
    m = bsz * s
    depth = w_in.shape[0]
    mem2 = mem.reshape(-1, d)
    h = x.reshape(m, d)
    segs = (SEG_Z, SEG_XBC, SEG_DT, SEG_Q, SEG_K, SEG_V, SEG_GSSD, SEG_GSB)
    seg_dtypes = (BF16, BF16, F32, BF16, BF16, BF16, BF16, BF16)
    w_in, w_ssd_o, w_sb_o, w_out, xa_wq, xa_wk, xa_wv, xa_wo = [
        cast_bf16(w) for w in (w_in, w_ssd_o, w_sb_o, w_out, xa_wq, xa_wk, xa_wv, xa_wo)]
    ffn_w = [cast_bf16(w) for w in (ffn_w_gate, ffn_w_up, ffn_w_down)]
    moe_w = [cast_bf16(w).reshape((-1,) + w.shape[2:]) for w in (moe_w_gate, moe_w_up, moe_w_down)]
    for i in range(depth):
        z, xbc, dt, q, k, v, g_ssd, g_sb = norm_proj(
            h, norm_mix[i], _in_proj_weight(w_in[i]), segs, seg_dtypes, "in_proj")
        sh = lambda a: a.reshape(bsz, s, a.shape[-1])
        y_ssd = ssd_branch(sh(z), sh(xbc), sh(dt), conv_w[i], conv_b[i], dt_bias[i], a_log[i],
                           d_skip[i], ssd_norm[i])
        y_sb = stick_breaking(sh(q), sh(k), sh(v))
        h = merge_mixers(h, y_ssd.reshape(m, -1), y_sb.reshape(m, -1), g_ssd, g_sb,
                         w_ssd_o[i], w_sb_o[i], w_out[i])
        w_kv = jnp.concatenate([xa_wk[i], xa_wv[i]], axis=1)
        (kv,) = norm_proj(mem2, norm_mem[i], w_kv, ((0, 2 * XA_WIDTH),), (BF16,), "mem_kv")
        h = cross_attention(h.reshape(bsz, s, d), norm_xa[i], xa_wq[i],
                            kv.reshape(bsz, -1, 2 * XA_WIDTH), xa_wo[i]).reshape(m, d)
        j = i // 2
        last = i == depth - 1
        if i % 2 == 0:
            h = dense_ffn(h, norm_ffn[i], *ffn_w, j)
            if last:
                h = final_norm_rows(h, final_norm)
        else:
            h = moe_ffn(h, norm_ffn[i], moe_router[j], *moe_w, j, final_norm, last)
    return h.reshape(bsz, s, d)
```

```python
import functools

import jax
import jax.numpy as jnp
from jax import lax
from jax.experimental import pallas as pl
from jax.experimental.pallas import tpu as pltpu

F32 = jnp.float32
BF16 = jnp.bfloat16

D_MODEL = 1024
SSD_HEAD_DIM = 64
SSD_INNER = 1024
SSD_HEADS = 16
SSD_GROUPS = 2
SSD_STATE = 64
SSD_CONV = 4
SSD_CHUNK = 128
SSD_CONV_DIM = SSD_INNER + 2 * SSD_GROUPS * SSD_STATE
SB_HEADS = 8
SB_HEAD_DIM = 64
SB_WIDTH = 512
SB_BLOCK = 128
XA_HEADS = 4
XA_HEAD_DIM = 128
XA_WIDTH = 512
D_FF = 2816
N_EXPERTS = 8
TOP_K = 2
EPS = 1e-6

LANES = 128
VMEM_LIMIT = 56 * 1024 * 1024
ROW_TILE = 512
PROJ_CHUNK = 512
FF_CHUNK = 256
MOE_TILE = 512


def _segments(widths):
    starts = [sum(widths[:i]) for i in range(len(widths))]
    return tuple(zip(starts, widths))


SEG_Z, SEG_XBC, SEG_DT, SEG_Q, SEG_K, SEG_V, SEG_GSSD, SEG_GSB = _segments(
    (SSD_INNER, SSD_CONV_DIM, LANES, SB_WIDTH, SB_WIDTH, SB_WIDTH, D_MODEL, D_MODEL))


def _cparams(sem):
    return pltpu.CompilerParams(dimension_semantics=sem, vmem_limit_bytes=VMEM_LIMIT)


def _rms(x, g):
    return x * lax.rsqrt(jnp.mean(x * x, axis=-1, keepdims=True) + EPS) * g


def _split2(x):
    hi = x.astype(BF16)
    lo = (x - hi.astype(F32)).astype(BF16)
    return hi, lo


def _dot(a, b):
    return jnp.dot(a, b, preferred_element_type=F32)


def _dot_nt(a, b):
    return lax.dot_general(a, b, (((1,), (1,)), ((), ())), preferred_element_type=F32)


def _dot_tn(a, b):
    return lax.dot_general(a, b, (((0,), (0,)), ((), ())), preferred_element_type=F32)


def _dot_split(x_f32, m_bf16):
    hi, lo = _split2(x_f32)
    return _dot(hi, m_bf16) + _dot(lo, m_bf16)


def _silu(x):
    return x / (1.0 + jnp.exp(-x))


def _sigmoid(x):
    return 1.0 / (1.0 + jnp.exp(-x))


def _norm_proj_kernel(x_ref, g_ref, w_ref, *out_refs, segs):
    x = x_ref[...]
    u = _rms(x, g_ref[...]).astype(BF16)
    for (start, width), o_ref in zip(segs, out_refs):
        for c0 in range(0, width, PROJ_CHUNK):
            cw = min(PROJ_CHUNK, width - c0)
            r = _dot(u, w_ref[:, start + c0:start + c0 + cw])
            o_ref[:, c0:c0 + cw] = r.astype(o_ref.dtype)


def norm_proj(x, g, w, segs, dtypes, name):
    m, k = x.shape
    tm = min(ROW_TILE, m)
    n = w.shape[1]
    out_shape = [jax.ShapeDtypeStruct((m, wd), dt) for (_, wd), dt in zip(segs, dtypes)]
    out_specs = [pl.BlockSpec((tm, wd), lambda i: (i, 0)) for (_, wd) in segs]
    return pl.pallas_call(
        functools.partial(_norm_proj_kernel, segs=tuple(segs)),
        grid=(m // tm,),
        in_specs=[pl.BlockSpec((tm, k), lambda i: (i, 0)),
                  pl.BlockSpec((1, k), lambda i: (0, 0)),
                  pl.BlockSpec((k, n), lambda i: (0, 0))],
        out_specs=out_specs,
        out_shape=out_shape,
        compiler_params=_cparams(("parallel",)),
        name=name,
    )(x, g.reshape(1, k), w)


def _ssd_kernel(z_ref, xbc_ref, dt_ref, cw_ref, cb_ref, dtb_ref, aneg_ref, dsk_ref,
                ng_ref, e_ref, o_ref, halo_scr, state_scr, y_scr):
    L = SSD_CHUNK
    c = pl.program_id(1)

    @pl.when(c == 0)
    def _():
        halo_scr[...] = jnp.zeros_like(halo_scr)
        state_scr[...] = jnp.zeros_like(state_scr)

    x_cur = xbc_ref[0].astype(F32)
    halo = halo_scr[...]
    sub = lax.broadcasted_iota(jnp.int32, (8, SSD_CONV_DIM), 0)
    acc = cb_ref[...] + x_cur * cw_ref[SSD_CONV - 1:SSD_CONV, :]
    for d in range(1, SSD_CONV):
        rolled = pltpu.roll(x_cur, d, axis=0)
        head = jnp.where(sub < d, pltpu.roll(halo, d, axis=0), rolled[0:8, :])
        shifted = jnp.concatenate([head, rolled[8:, :]], axis=0)
        acc = acc + shifted * cw_ref[SSD_CONV - 1 - d:SSD_CONV - d, :]
    halo_scr[...] = x_cur[L - 8:L, :]
    xc = _silu(acc)
    xs = xc[:, :SSD_INNER]
    b128 = xc[:, SSD_INNER:SSD_INNER + LANES].astype(BF16)
    c128 = xc[:, SSD_INNER + LANES:SSD_INNER + 2 * LANES]

    row = lax.broadcasted_iota(jnp.int32, (L, L), 0)
    col = lax.broadcasted_iota(jnp.int32, (L, L), 1)
    causal = col <= row
    tril = jnp.where(causal, 1.0, 0.0).astype(BF16)

    dt_in = dt_ref[0] + dtb_ref[...]
    dt = jnp.maximum(dt_in, 0.0) + jnp.log(1.0 + jnp.exp(-jnp.abs(dt_in)))
    a = dt * aneg_ref[...]
    a_hi = a.astype(BF16)
    a_mid = (a - a_hi.astype(F32))
    a_mid_b = a_mid.astype(BF16)
    a_lo = (a_mid - a_mid_b.astype(F32)).astype(BF16)
    acum = _dot(tril, a_hi) + _dot(tril, a_mid_b) + _dot(tril, a_lo)
    acum_t = acum.T
    alast = acum[L - 1:L, :]

    expand = e_ref[...]
    ea_e = _dot_split(jnp.exp(acum), expand)
    w_e = _dot_split(dt * jnp.exp(alast - acum), expand)
    cd_e = _dot_split(jnp.broadcast_to(jnp.exp(alast), (8, LANES)), expand)[0:1, :]
    dt_t = dt.T

    x_b = xs.astype(BF16)
    xw_b = (xs * w_e).astype(BF16)

    lane_half = lax.broadcasted_iota(jnp.int32, (L, LANES), 1) < SSD_STATE
    zero_b = jnp.zeros((L, LANES), BF16)
    c128_b = c128.astype(BF16)
    cb_g = [
        _dot_nt(jnp.where(lane_half, c128_b, zero_b), b128),
        _dot_nt(jnp.where(lane_half, zero_b, c128_b), b128),
    ]
    heads_per_group = SSD_HEADS // SSD_GROUPS
    for p in range(SSD_HEADS // 2):
        xp = x_b[:, p * LANES:(p + 1) * LANES]
        y_pair = None
        for s in range(2):
            h = 2 * p + s
            seg = acum[:, h:h + 1] - acum_t[h:h + 1, :]
            decay = jnp.exp(jnp.where(causal, seg, -jnp.inf))
            m_h = (cb_g[h // heads_per_group] * decay * dt_t[h:h + 1, :]).astype(BF16)
            x_h = jnp.where(lane_half, xp, zero_b) if s == 0 else jnp.where(lane_half, zero_b, xp)
            t = _dot(m_h, x_h)
            y_pair = t if y_pair is None else y_pair + t
        y_scr[:, p * LANES:(p + 1) * LANES] = y_pair

    state = state_scr[...]
    y_off = _dot(c128_b, state.astype(BF16)) * ea_e
    s_new = _dot_tn(b128, xw_b)
    srow = lax.broadcasted_iota(jnp.int32, (LANES, SSD_INNER), 0) < SSD_STATE
    scol = lax.broadcasted_iota(jnp.int32, (LANES, SSD_INNER), 1) < SSD_INNER // SSD_GROUPS
    state_scr[...] = state * cd_e + jnp.where(srow == scol, s_new, 0.0)

    y = y_scr[...] + y_off + xs * dsk_ref[...]
    y = y * _silu(z_ref[0].astype(F32))
    gw = SSD_INNER // SSD_GROUPS
    for g in range(SSD_GROUPS):
        yg = y[:, g * gw:(g + 1) * gw]
        o_ref[0, :, g * gw:(g + 1) * gw] = _rms(yg, ng_ref[:, g * gw:(g + 1) * gw]).astype(o_ref.dtype)


def ssd_branch(z, xbc, dt, conv_w, conv_b, dt_bias, a_log, d_skip, norm_g):
    bsz, s, _ = z.shape
    nc = s // SSD_CHUNK
    pad = LANES - SSD_HEADS
    dtb = jnp.pad(dt_bias.astype(F32), (0, pad)).reshape(1, LANES)
    aneg = jnp.pad(-jnp.exp(a_log.astype(F32)), (0, pad)).reshape(1, LANES)
    dsk = jnp.repeat(d_skip.astype(F32), SSD_HEAD_DIM).reshape(1, SSD_INNER)
    expand = (jnp.arange(LANES)[:, None] == (jnp.arange(SSD_INNER)[None, :] // SSD_HEAD_DIM)).astype(BF16)
    const = lambda shape: pl.BlockSpec(shape, lambda b, c: (0,) * len(shape))
    return pl.pallas_call(
        _ssd_kernel,
        grid=(bsz, nc),
        in_specs=[pl.BlockSpec((1, SSD_CHUNK, SSD_INNER), lambda b, c: (b, c, 0)),
                  pl.BlockSpec((1, SSD_CHUNK, SSD_CONV_DIM), lambda b, c: (b, c, 0)),
                  pl.BlockSpec((1, SSD_CHUNK, LANES), lambda b, c: (b, c, 0)),
                  const((SSD_CONV, SSD_CONV_DIM)),
                  const((1, SSD_CONV_DIM)),
                  const((1, LANES)),
                  const((1, LANES)),
                  const((1, SSD_INNER)),
                  const((1, SSD_INNER)),
                  const((LANES, SSD_INNER))],
        out_specs=pl.BlockSpec((1, SSD_CHUNK, SSD_INNER), lambda b, c: (b, c, 0)),
        out_shape=jax.ShapeDtypeStruct((bsz, s, SSD_INNER), BF16),
        scratch_shapes=[pltpu.VMEM((8, SSD_CONV_DIM), F32),
                        pltpu.VMEM((LANES, SSD_INNER), F32),
                        pltpu.VMEM((SSD_CHUNK, SSD_INNER), F32)],
        compiler_params=_cparams(("parallel", "arbitrary")),
        name="ssd_scan",
    )(z, xbc, dt, conv_w.astype(F32), conv_b.astype(F32).reshape(1, -1), dtb, aneg, dsk,
      norm_g.astype(F32).reshape(1, -1), expand)


SB_PAIRS = SB_WIDTH // LANES
EXP_UNDERFLOW = -104.0


def _sb_kernel(q_ref, k_ref, v_ref, nsuffix_ref, o_ref, run_scr, acc_scr, z_scr):
    T = SB_BLOCK
    qi = pl.program_id(1)
    nsuffix = nsuffix_ref[...]
    rw = lax.broadcasted_iota(jnp.int32, (T, 2 * T), 0)
    cw = lax.broadcasted_iota(jnp.int32, (T, 2 * T), 1)
    strict_w = (cw & (T - 1)) < rw
    strict = strict_w[:, :T]
    q_all = q_ref[0] * jnp.asarray(SB_HEAD_DIM ** -0.5, BF16)
    pairs = range(SB_PAIRS)
    lane_lo = lax.broadcasted_iota(jnp.int32, (T, LANES), 1) < SB_HEAD_DIM
    zero_b = jnp.zeros((T, LANES), BF16)

    run_scr[...] = jnp.zeros_like(run_scr)
    acc_scr[...] = jnp.zeros_like(acc_scr)

    def pair_rows(ref, j, p):
        x = ref[0, pl.ds(pl.multiple_of(j * T, T), T), p * LANES:(p + 1) * LANES]
        return jnp.concatenate([jnp.where(lane_lo, x, zero_b), jnp.where(lane_lo, zero_b, x)], axis=0)

    def scores(j):
        for p in pairs:
            z_scr[p] = _dot_nt(q_all[:, p * LANES:(p + 1) * LANES], pair_rows(k_ref, j, p))

    scores(qi)

    def block(j, diag):
        runmax = None
        sps, lsigs = [], []
        for p in pairs:
            z = z_scr[p]
            sp = jnp.maximum(z, 0.0) + jnp.log(1.0 + jnp.exp(-jnp.abs(z)))
            lsigs.append(z - sp)
            sps.append(jnp.where(strict_w, sp, 0.0) if diag else sp)
        lbs = [_dot(sps[p].astype(BF16), nsuffix) for p in pairs]
        scores(jnp.maximum(j - 1, 0))
        for p in pairs:
            args = lsigs[p] + lbs[p]
            atts = []
            for h in range(2):
                run = run_scr[2 * p + h]
                att = jnp.exp(args[:, h * T:(h + 1) * T] + run)
                if diag:
                    att = jnp.where(strict, att, 0.0)
                atts.append(att.astype(BF16))
                run_new = run[:, 0:1] - jnp.sum(sps[p][:, h * T:(h + 1) * T], axis=-1, keepdims=True)
                run_scr[2 * p + h] = jnp.broadcast_to(run_new, (T, LANES))
                runmax = run_new if runmax is None else jnp.maximum(runmax, run_new)
            acc_scr[p] += _dot(jnp.concatenate(atts, axis=1), pair_rows(v_ref, j, p))
        return (jnp.max(runmax) > EXP_UNDERFLOW).astype(jnp.int32)

    go = block(qi, True)

    def body(carry):
        i, _ = carry
        return i + 1, block(qi - i, False)

    lax.while_loop(lambda c: (c[0] <= qi) & (c[1] > 0), body, (jnp.int32(1), go))
    for p in range(SB_PAIRS):
        o_ref[0, :, p * LANES:(p + 1) * LANES] = acc_scr[p].astype(o_ref.dtype)


def stick_breaking(q, k, v):
    bsz, s, _ = q.shape
    nq = s // SB_BLOCK
    idx = jnp.arange(2 * SB_BLOCK)
    same_head = (idx[:, None] < SB_BLOCK) == (idx[None, :] < SB_BLOCK)
    nsuffix = jnp.where(same_head & (idx[:, None] > idx[None, :]), -1.0, 0.0).astype(BF16)
    return pl.pallas_call(
        _sb_kernel,
        grid=(bsz, nq),
        in_specs=[pl.BlockSpec((1, SB_BLOCK, SB_WIDTH), lambda b, i: (b, i, 0)),
                  pl.BlockSpec((1, s, SB_WIDTH), lambda b, i: (b, 0, 0)),
                  pl.BlockSpec((1, s, SB_WIDTH), lambda b, i: (b, 0, 0)),
                  pl.BlockSpec((2 * SB_BLOCK, 2 * SB_BLOCK), lambda b, i: (0, 0))],
        out_specs=pl.BlockSpec((1, SB_BLOCK, SB_WIDTH), lambda b, i: (b, i, 0)),
        out_shape=jax.ShapeDtypeStruct((bsz, s, SB_WIDTH), BF16),
        scratch_shapes=[pltpu.VMEM((SB_HEADS, SB_BLOCK, LANES), F32),
                        pltpu.VMEM((SB_PAIRS, SB_BLOCK, LANES), F32),
                        pltpu.VMEM((SB_PAIRS, SB_BLOCK, 2 * SB_BLOCK), F32)],
        compiler_params=_cparams(("parallel", "arbitrary")),
        name="stick_breaking",
    )(q, k, v, nsuffix)


def _merge_kernel(h_ref, ys_ref, yb_ref, gs_ref, gb_ref, ws_ref, wb_ref, wo_ref, o_ref):
    ms = _sigmoid(gs_ref[...].astype(F32)) * _dot(ys_ref[...], ws_ref[...])
    mb = _sigmoid(gb_ref[...].astype(F32)) * _dot(yb_ref[...], wb_ref[...])
    merged = (ms + mb).astype(BF16)
    o_ref[...] = h_ref[...] + _dot(merged, wo_ref[...])


def merge_mixers(h, y_ssd, y_sb, g_ssd, g_sb, w_ssd_o, w_sb_o, w_out):
    m, d = h.shape
    tm = min(ROW_TILE, m)
    rowspec = lambda w: pl.BlockSpec((tm, w), lambda i: (i, 0))
    wspec = lambda r, c: pl.BlockSpec((r, c), lambda i: (0, 0))
    return pl.pallas_call(
        _merge_kernel,
        grid=(m // tm,),
        in_specs=[rowspec(d), rowspec(SSD_INNER), rowspec(SB_WIDTH), rowspec(d), rowspec(d),
                  wspec(SSD_INNER, d), wspec(SB_WIDTH, d), wspec(d, d)],
        out_specs=rowspec(d),
        out_shape=jax.ShapeDtypeStruct((m, d), F32),
        compiler_params=_cparams(("parallel",)),
        name="merge_mixers",
    )(h, y_ssd, y_sb, g_ssd, g_sb, w_ssd_o, w_sb_o, w_out)


def _xattn_kernel(h_ref, g_ref, wq_ref, kv_ref, wo_ref, o_ref):
    h = h_ref[0]
    u = _rms(h, g_ref[...]).astype(BF16)
    q = _dot(u, wq_ref[...]).astype(BF16)
    scale = XA_HEAD_DIM ** -0.5
    outs = []
    for hd in range(XA_HEADS):
        lo = hd * XA_HEAD_DIM
        kh = kv_ref[0, :, lo:lo + XA_HEAD_DIM]
        vh = kv_ref[0, :, XA_WIDTH + lo:XA_WIDTH + lo + XA_HEAD_DIM]
        s = _dot_nt(q[:, lo:lo + XA_HEAD_DIM], kh) * scale
        e = jnp.exp(s - jnp.max(s, axis=-1, keepdims=True))
        oh = _dot(e.astype(BF16), vh) / jnp.sum(e, axis=-1, keepdims=True)
        outs.append(oh.astype(BF16))
    o = jnp.concatenate(outs, axis=-1)
    o_ref[0] = h + _dot(o, wo_ref[...])


def cross_attention(h, g, wq, kv, wo):
    bsz, s, d = h.shape
    tm = min(ROW_TILE, s)
    mem_len = kv.shape[1]
    return pl.pallas_call(
        _xattn_kernel,
        grid=(bsz, s // tm),
        in_specs=[pl.BlockSpec((1, tm, d), lambda b, i: (b, i, 0)),
                  pl.BlockSpec((1, d), lambda b, i: (0, 0)),
                  pl.BlockSpec((d, XA_WIDTH), lambda b, i: (0, 0)),
                  pl.BlockSpec((1, mem_len, 2 * XA_WIDTH), lambda b, i: (b, 0, 0)),
                  pl.BlockSpec((XA_WIDTH, d), lambda b, i: (0, 0))],
        out_specs=pl.BlockSpec((1, tm, d), lambda b, i: (b, i, 0)),
        out_shape=jax.ShapeDtypeStruct((bsz, s, d), F32),
        compiler_params=_cparams(("parallel", "parallel")),
        name="cross_attention",
    )(h, g.reshape(1, d), wq, kv, wo)


def _swiglu_tile(u, wg_ref, wu_ref, wd_ref):
    acc = None
    for c0 in range(0, D_FF, FF_CHUNK):
        gt = _dot(u, wg_ref[0, :, c0:c0 + FF_CHUNK])
        up = _dot(u, wu_ref[0, :, c0:c0 + FF_CHUNK])
        hid = (_silu(gt) * up).astype(BF16)
        t = _dot(hid, wd_ref[0, c0:c0 + FF_CHUNK, :])
        acc = t if acc is None else acc + t
    return acc


def _dense_ffn_kernel(h_ref, g_ref, wg_ref, wu_ref, wd_ref, o_ref):
    h = h_ref[...]
    u = _rms(h, g_ref[...]).astype(BF16)
    o_ref[...] = h + _swiglu_tile(u, wg_ref, wu_ref, wd_ref)


def dense_ffn(h, g, wg, wu, wd, layer):
    m, d = h.shape
    tm = min(ROW_TILE, m)
    return pl.pallas_call(
        _dense_ffn_kernel,
        grid=(m // tm,),
        in_specs=[pl.BlockSpec((tm, d), lambda i: (i, 0)),
                  pl.BlockSpec((1, d), lambda i: (0, 0)),
                  pl.BlockSpec((1, d, D_FF), lambda i: (layer, 0, 0)),
                  pl.BlockSpec((1, d, D_FF), lambda i: (layer, 0, 0)),
                  pl.BlockSpec((1, D_FF, d), lambda i: (layer, 0, 0))],
        out_specs=pl.BlockSpec((tm, d), lambda i: (i, 0)),
        out_shape=jax.ShapeDtypeStruct((m, d), F32),
        compiler_params=_cparams(("parallel",)),
        name="dense_ffn",
    )(h, g.reshape(1, d), wg, wu, wd)


def _expert_ffn_kernel(te_ref, nt_ref, x_ref, wg_ref, wu_ref, wd_ref, o_ref):
    @pl.when(pl.program_id(0) < nt_ref[0])
    def _():
        o_ref[...] = _swiglu_tile(x_ref[...], wg_ref, wu_ref, wd_ref).astype(o_ref.dtype)

    @pl.when(pl.program_id(0) >= nt_ref[0])
    def _():
        o_ref[...] = jnp.zeros_like(o_ref)


def expert_ffn(x_sorted, tile_expert, n_tiles_used, wg, wu, wd, tm):
    r, d = x_sorted.shape
    grid_spec = pltpu.PrefetchScalarGridSpec(
        num_scalar_prefetch=2,
        grid=(r // tm,),
        in_specs=[pl.BlockSpec((tm, d), lambda i, te, nt: (i, 0)),
                  pl.BlockSpec((1, d, D_FF), lambda i, te, nt: (te[i], 0, 0)),
                  pl.BlockSpec((1, d, D_FF), lambda i, te, nt: (te[i], 0, 0)),
                  pl.BlockSpec((1, D_FF, d), lambda i, te, nt: (te[i], 0, 0))],
        out_specs=pl.BlockSpec((tm, d), lambda i, te, nt: (i, 0)),
    )
    return pl.pallas_call(
        _expert_ffn_kernel,
        grid_spec=grid_spec,
        out_shape=jax.ShapeDtypeStruct((r, d), BF16),
        compiler_params=_cparams(("arbitrary",)),
        name="expert_ffn",
    )(tile_expert, n_tiles_used, x_sorted, wg, wu, wd)


def _router_kernel(h_ref, g_ref, wr_hi_ref, wr_lo_ref, earlier_ref, u_ref, route_ref, fields_ref, counts_ref):
    uf = _rms(h_ref[...], g_ref[...])
    u_hi = uf.astype(BF16)
    u_ref[...] = u_hi
    u_lo = (uf - u_hi.astype(F32)).astype(BF16)
    logits = _dot(u_hi, wr_hi_ref[...]) + (_dot(u_lo, wr_hi_ref[...]) + _dot(u_hi, wr_lo_ref[...]))
    lane = lax.broadcasted_iota(jnp.int32, logits.shape, 1)
    logits = jnp.where(lane < N_EXPERTS, logits, -jnp.inf)
    m1 = jnp.max(logits, axis=-1, keepdims=True)
    i1 = jnp.min(jnp.where(logits == m1, lane, LANES), axis=-1, keepdims=True)
    rest = jnp.where(lane == i1, -jnp.inf, logits)
    m2 = jnp.max(rest, axis=-1, keepdims=True)
    i2 = jnp.min(jnp.where(rest == m2, lane, LANES), axis=-1, keepdims=True)
    e2 = jnp.exp(m2 - m1)
    w1 = 1.0 / (1.0 + e2)
    w2 = e2 / (1.0 + e2)
    route = jnp.where(lane == 0, w1, jnp.where(lane == 1, w2, 0.0))
    route = jnp.where(lane == 2, i1.astype(F32), route)
    route = jnp.where(lane == 3, i2.astype(F32), route)
    tm = logits.shape[0]
    picked = jnp.where(lane == i1, 1.0, jnp.where(lane == i2, 1.0, 0.0))
    before = _dot(earlier_ref[...], picked.astype(BF16))
    rank1 = jnp.sum(jnp.where(lane == i1, before, 0.0), axis=-1, keepdims=True)
    rank2 = jnp.sum(jnp.where(lane == i2, before, 0.0), axis=-1, keepdims=True)
    route = jnp.where(lane == 4, rank1, route)
    route = jnp.where(lane == 5, rank2, route)
    route_ref[...] = route
    for b in range(tm // LANES):
        fields_ref[:, b * LANES:(b + 1) * LANES] = route[b * LANES:(b + 1) * LANES, :].T[0:8, :]
    counts_ref[...] = jnp.broadcast_to(jnp.sum(picked, axis=0, keepdims=True), counts_ref.shape)


def router(h, g, w_router):
    m, d = h.shape
    tm = min(ROW_TILE, m)
    nt = m // tm
    wr = jnp.pad(w_router.astype(F32), ((0, 0), (0, LANES - N_EXPERTS)))
    wr_hi = wr.astype(BF16)
    wr_lo = (wr - wr_hi.astype(F32)).astype(BF16)
    earlier = (jnp.arange(tm)[None, :] < jnp.arange(tm)[:, None]).astype(BF16)
    return pl.pallas_call(
        _router_kernel,
        grid=(nt,),
        in_specs=[pl.BlockSpec((tm, d), lambda i: (i, 0)),
                  pl.BlockSpec((1, d), lambda i: (0, 0)),
                  pl.BlockSpec((d, LANES), lambda i: (0, 0)),
                  pl.BlockSpec((d, LANES), lambda i: (0, 0)),
                  pl.BlockSpec((tm, tm), lambda i: (0, 0))],
        out_specs=[pl.BlockSpec((tm, d), lambda i: (i, 0)),
                   pl.BlockSpec((tm, LANES), lambda i: (i, 0)),
                   pl.BlockSpec((8, tm), lambda i: (0, i)),
                   pl.BlockSpec((8, LANES), lambda i: (i, 0))],
        out_shape=[jax.ShapeDtypeStruct((m, d), BF16),
                   jax.ShapeDtypeStruct((m, LANES), F32),
                   jax.ShapeDtypeStruct((8, m), F32),
                   jax.ShapeDtypeStruct((8 * nt, LANES), F32)],
        compiler_params=_cparams(("parallel",)),
        name="router",
    )(h, g.reshape(1, d), wr_hi, wr_lo, earlier)


GRAN = 16
LOC_ROWS = TOP_K * ROW_TILE + N_EXPERTS * GRAN
MAX_GRAN = LOC_ROWS // GRAN


def _dispatch_kernel(gdst_ref, ngran_ref, ls_ref, zdst_ref, nused_ref, u_ref, f_ref, xout_ref, xs_scr, zero_scr,
                     sem):
    t = pl.program_id(0)
    nt = pl.num_programs(0)
    tm = u_ref.shape[0]
    f = f_ref[...]

    def zero_piece(k):
        dst = xout_ref.at[pl.ds(pl.multiple_of(jnp.maximum(zdst_ref[k], 0), GRAN), GRAN), :]
        return pltpu.make_async_copy(zero_scr.at[0:GRAN, :], dst, sem.at[1])

    def zero_tile(i):
        dst = xout_ref.at[pl.ds(pl.multiple_of(i * tm, tm), tm), :]
        return pltpu.make_async_copy(zero_scr, dst, sem.at[1])

    n_row_tiles = xout_ref.shape[0] // tm

    def for_zero_pieces(fn):
        def body(k, c):
            @pl.when(zdst_ref[k] >= 0)
            def _():
                fn(zero_piece(k))
            return c
        lax.fori_loop(0, zdst_ref.shape[0], body, 0)

    @pl.when(t == 0)
    def _():
        zero_scr[...] = jnp.zeros_like(zero_scr)
        for_zero_pieces(lambda cp: cp.start())
        lax.fori_loop(nused_ref[0], n_row_tiles, lambda i, c: (zero_tile(i).start(), c)[1], 0)

    def local_row(expert_row, rank_row):
        base = jnp.zeros_like(rank_row)
        for e in range(N_EXPERTS):
            base = jnp.where(expert_row == float(e), ls_ref[t * N_EXPERTS + e].astype(F32), base)
        return (base + rank_row).astype(jnp.int32)

    lp1 = local_row(f[2:3, :], f[4:5, :])
    lp2 = local_row(f[3:4, :], f[5:6, :])
    r = lax.broadcasted_iota(jnp.int32, (LOC_ROWS, tm), 0)
    sel = jnp.where(r == lp1, 1.0, jnp.where(r == lp2, 1.0, 0.0)).astype(BF16)
    xs = _dot(sel, u_ref[...]).astype(BF16)

    def piece(tile, q):
        src = xs_scr.at[pl.ds(pl.multiple_of(q * GRAN, GRAN), GRAN), :]
        dst = xout_ref.at[pl.ds(pl.multiple_of(gdst_ref[tile * MAX_GRAN + q], GRAN), GRAN), :]
        return pltpu.make_async_copy(src, dst, sem.at[0])

    def wait_all(tile):
        lax.fori_loop(0, ngran_ref[tile], lambda q, c: (piece(tile, q).wait(), c)[1], 0)

    @pl.when(t > 0)
    def _():
        wait_all(t - 1)

    xs_scr[...] = xs
    lax.fori_loop(0, ngran_ref[t], lambda q, c: (piece(t, q).start(), c)[1], 0)

    @pl.when(t == nt - 1)
    def _():
        wait_all(t)
        for_zero_pieces(lambda cp: cp.wait())
        lax.fori_loop(nused_ref[0], n_row_tiles, lambda i, c: (zero_tile(i).wait(), c)[1], 0)


def dispatch(u, fields, gdst, ngran, seg_ls, zdst, n_used, n_rows):
    m, d = u.shape
    tm = min(ROW_TILE, m)
    grid_spec = pltpu.PrefetchScalarGridSpec(
        num_scalar_prefetch=5,
        grid=(m // tm,),
        in_specs=[pl.BlockSpec((tm, d), lambda i, *_: (i, 0)),
                  pl.BlockSpec((8, tm), lambda i, *_: (0, i))],
        out_specs=pl.BlockSpec(memory_space=pl.ANY),
        scratch_shapes=[pltpu.VMEM((LOC_ROWS, d), BF16), pltpu.VMEM((tm, d), BF16),
                        pltpu.SemaphoreType.DMA((2,))],
    )
    return pl.pallas_call(
        _dispatch_kernel,
        grid_spec=grid_spec,
        out_shape=jax.ShapeDtypeStruct((n_rows, d), BF16),
        compiler_params=_cparams(("arbitrary",)),
        name="moe_dispatch",
    )(gdst, ngran, seg_ls, zdst, n_used, u, fields)


def _combine_kernel(gdst_ref, ngran_ref, h_ref, route_ref, ls_ref, g_ref, y_ref, o_ref, y_scr, sem,
                    *, final_norm):
    t = pl.program_id(0)
    nt = pl.num_programs(0)
    tm = h_ref.shape[0]
    slot = lax.rem(t, 2)

    def piece(tile, q):
        buf = lax.rem(tile, 2)
        src = y_ref.at[pl.ds(pl.multiple_of(gdst_ref[tile * MAX_GRAN + q], GRAN), GRAN), :]
        dst = y_scr.at[buf, pl.ds(pl.multiple_of(q * GRAN, GRAN), GRAN), :]
        return pltpu.make_async_copy(src, dst, sem.at[buf])

    def fetch(tile):
        lax.fori_loop(0, ngran_ref[tile], lambda q, c: (piece(tile, q).start(), c)[1], 0)

    @pl.when(t == 0)
    def _():
        y_scr[...] = jnp.zeros_like(y_scr)
        fetch(t)

    @pl.when(t + 1 < nt)
    def _():
        fetch(t + 1)

    route = route_ref[...]
    lane = lax.broadcasted_iota(jnp.int32, (tm, LANES), 1)
    ls_row = ls_ref[0:1, :]

    def local_row(expert_col, rank_col):
        base = jnp.sum(jnp.where(lane == expert_col.astype(jnp.int32), ls_row, 0.0), axis=-1, keepdims=True)
        return (base + rank_col).astype(jnp.int32)

    c = lax.broadcasted_iota(jnp.int32, (tm, LOC_ROWS), 1)
    sel1 = jnp.where(c == local_row(route[:, 2:3], route[:, 4:5]), 1.0, 0.0).astype(BF16)
    sel2 = jnp.where(c == local_row(route[:, 3:4], route[:, 5:6]), 1.0, 0.0).astype(BF16)

    lax.fori_loop(0, ngran_ref[t], lambda q, c_: (piece(t, q).wait(), c_)[1], 0)
    y = y_scr[slot]
    out = h_ref[...] + (route[:, 0:1] * _dot(sel1, y) + route[:, 1:2] * _dot(sel2, y))
    if final_norm:
        out = _rms(out, g_ref[...])
    o_ref[...] = out


def combine(h, y_sorted, route, ls_rows, gdst, ngran, g, final_norm):
    m, d = h.shape
    tm = min(ROW_TILE, m)
    grid_spec = pltpu.PrefetchScalarGridSpec(
        num_scalar_prefetch=2,
        grid=(m // tm,),
        in_specs=[pl.BlockSpec((tm, d), lambda i, *_: (i, 0)),
                  pl.BlockSpec((tm, LANES), lambda i, *_: (i, 0)),
                  pl.BlockSpec((8, LANES), lambda i, *_: (i, 0)),
                  pl.BlockSpec((1, d), lambda i, *_: (0, 0)),
                  pl.BlockSpec(memory_space=pl.ANY)],
        out_specs=pl.BlockSpec((tm, d), lambda i, *_: (i, 0)),
        scratch_shapes=[pltpu.VMEM((2, LOC_ROWS, d), BF16), pltpu.SemaphoreType.DMA((2,))],
    )
    return pl.pallas_call(
        functools.partial(_combine_kernel, final_norm=final_norm),
        grid_spec=grid_spec,
        out_shape=jax.ShapeDtypeStruct((m, d), F32),
        compiler_params=_cparams(("arbitrary",)),
        name="moe_combine",
    )(gdst, ngran, h, route, ls_rows, g.reshape(1, d), y_sorted)


def _final_norm_kernel(h_ref, g_ref, o_ref):
    o_ref[...] = _rms(h_ref[...], g_ref[...])


def final_norm_rows(h, g):
    m, d = h.shape
    tm = min(ROW_TILE, m)
    return pl.pallas_call(
        _final_norm_kernel,
        grid=(m // tm,),
        in_specs=[pl.BlockSpec((tm, d), lambda i: (i, 0)), pl.BlockSpec((1, d), lambda i: (0, 0))],
        out_specs=pl.BlockSpec((tm, d), lambda i: (i, 0)),
        out_shape=jax.ShapeDtypeStruct((m, d), F32),
        compiler_params=_cparams(("parallel",)),
        name="final_norm",
    )(h, g.reshape(1, d))


def moe_ffn(h, g, w_router, wg, wu, wd, layer, g_final, apply_final_norm):
    m, d = h.shape
    tm = min(MOE_TILE, m)
    nt = m // tm
    u, route, fields, counts = router(h, g, w_router)
    counts = counts.reshape(nt, 8, LANES)[:, 0, :N_EXPERTS].astype(jnp.int32)
    seg = (counts + GRAN - 1) // GRAN * GRAN
    loc_end = jnp.cumsum(seg, axis=1)
    loc_start = loc_end - seg
    group_tiles = (jnp.sum(seg, axis=0) + tm - 1) // tm
    tile_end = jnp.cumsum(group_tiles)
    group_start = (tile_end - group_tiles) * tm
    seg_dst = group_start[None, :] + jnp.cumsum(seg, axis=0) - seg
    q = jnp.arange(MAX_GRAN, dtype=jnp.int32) * GRAN
    piece_expert = jnp.minimum(
        jnp.sum((q[None, :, None] >= loc_end[:, None, :]).astype(jnp.int32), axis=-1), N_EXPERTS - 1)
    owner = piece_expert[:, :, None] == jnp.arange(N_EXPERTS)[None, None, :]
    gdst = (jnp.sum(jnp.where(owner, (seg_dst - loc_start)[:, None, :], 0), axis=-1)
            + q[None, :]).reshape(-1).astype(jnp.int32)
    ngran = (loc_end[:, -1] // GRAN).astype(jnp.int32)
    n_rows = -(-(TOP_K * m + nt * N_EXPERTS * (GRAN - 1) + N_EXPERTS * (tm - 1)) // tm) * tm
    n_tiles = n_rows // tm
    tile_expert = jnp.minimum(
        jnp.sum((jnp.arange(n_tiles)[:, None] >= tile_end[None, :]).astype(jnp.int32), axis=1),
        N_EXPERTS - 1).astype(jnp.int32) + layer * N_EXPERTS
    n_used = tile_end[-1:].astype(jnp.int32)
    ls_rows = jnp.broadcast_to(
        jnp.pad(loc_start.astype(F32), ((0, 0), (0, LANES - N_EXPERTS)))[:, None, :], (nt, 8, LANES)
    ).reshape(nt * 8, LANES)
    group_rows = jnp.sum(seg, axis=0)
    zk = jnp.arange(tm // GRAN - 1, dtype=jnp.int32)[None, :] * GRAN
    zdst = jnp.where(zk < (group_tiles * tm - group_rows)[:, None],
                     (group_start + group_rows)[:, None] + zk, -1).reshape(-1).astype(jnp.int32)
    x_sorted = dispatch(u, fields, gdst, ngran, loc_start.reshape(-1).astype(jnp.int32), zdst, n_used, n_rows)
    y = expert_ffn(x_sorted, tile_expert, n_used, wg, wu, wd, tm)
    return combine(h, y, route, ls_rows, gdst, ngran, g_final, apply_final_norm)


def _cast_kernel(x_ref, o_ref):
    o_ref[...] = x_ref[...].astype(o_ref.dtype)


CAST_BLOCK_BYTES = 6 * 1024 * 1024


def cast_bf16(w):
    rows, cols = w.shape[-2:]
    lead = w.size // (rows * cols)
    tr = next(rows // k for k in range(1, rows // 16 + 1)
              if rows % k == 0 and (rows // k) % 16 == 0 and (rows // k) * cols * 4 <= CAST_BLOCK_BYTES)
    out = pl.pallas_call(
        _cast_kernel,
        grid=(lead, rows // tr),
        in_specs=[pl.BlockSpec((1, tr, cols), lambda l, i: (l, i, 0))],
        out_specs=pl.BlockSpec((1, tr, cols), lambda l, i: (l, i, 0)),
        out_shape=jax.ShapeDtypeStruct((lead, rows, cols), BF16),
        compiler_params=_cparams(("parallel", "parallel")),
        name="cast_bf16",
    )(w.reshape(lead, rows, cols))
    return out.reshape(w.shape)


def _in_proj_weight(w):
    o_dt = SSD_INNER + SSD_CONV_DIM
    o_qkv = o_dt + SSD_HEADS
    dt_cols = jnp.pad(w[:, o_dt:o_qkv], ((0, 0), (0, LANES - SSD_HEADS)))
    return jnp.concatenate([w[:, :o_dt], dt_cols, w[:, o_qkv:]], axis=1)


def kernel(x, mem, norm_mix, w_in, conv_w, conv_b, dt_bias, a_log, d_skip, ssd_norm, w_ssd_o, w_sb_o, w_out, norm_xa, norm_mem, xa_wq, xa_wk, xa_wv, xa_wo, norm_ffn, ffn_w_gate, ffn_w_up, ffn_w_down, moe_router, moe_w_gate, moe_w_up, moe_w_down, final_norm):
    bsz, s, d = x.shape
    m = bsz * s
    depth = w_in.shape[0]
    mem2 = mem.reshape(-1, d)
    h = x.reshape(m, d)
    segs = (SEG_Z, SEG_XBC, SEG_DT, SEG_Q, SEG_K, SEG_V, SEG_GSSD, SEG_GSB)
    seg_dtypes = (BF16, BF16, F32, BF16, BF16, BF16, BF16, BF16)
    w_in, w_ssd_o, w_sb_o, w_out, xa_wq, xa_wk, xa_wv, xa_wo = [
        cast_bf16(w) for w in (w_in, w_ssd_o, w_sb_o, w_out, xa_wq, xa_wk, xa_wv, xa_wo)]
    ffn_w = [cast_bf16(w) for w in (ffn_w_gate, ffn_w_up, ffn_w_down)]
    moe_w = [cast_bf16(w).reshape((-1,) + w.shape[2:]) for w in (moe_w_gate, moe_w_up, moe_w_down)]
    for i in range(depth):
        z, xbc, dt, q, k, v, g_ssd, g_sb = norm_proj(
            h, norm_mix[i], _in_proj_weight(w_in[i]), segs, seg_dtypes, "in_proj")
        sh = lambda a: a.reshape(bsz, s, a.shape[-1])
        y_ssd = ssd_branch(sh(z), sh(xbc), sh(dt), conv_w[i], conv_b[i], dt_bias[i], a_log[i],
                           d_skip[i], ssd_norm[i])
        y_sb = stick_breaking(sh(q), sh(k), sh(v))
        h = merge_mixers(h, y_ssd.reshape(m, -1), y_sb.reshape(m, -1), g_ssd, g_sb,
                         w_ssd_o[i], w_sb_o[i], w_out[i])
        w_kv = jnp.concatenate([xa_wk[i], xa_wv[i]], axis=1)
        (kv,) = norm_proj(mem2, norm_mem[i], w_kv, ((0, 2 * XA_WIDTH),), (BF16,), "mem_kv")
        h = cross_attention(h.reshape(bsz, s, d), norm_xa[i], xa_wq[i],
                            kv.reshape(bsz, -1, 2 * XA_WIDTH), xa_wo[i]).reshape(m, d)
        j = i // 2
        last = i == depth - 1
        if i % 2 == 0:
            h = dense_ffn(h, norm_ffn[i], *ffn_w, j)
            if last:
                h = final_norm_rows(h, final_norm)
        else:
            h = moe_ffn(h, norm_ffn[i], moe_router[j], *moe_w, j, final_norm, last)
    return h.reshape(bsz, s, d)
```

```python
import functools

import jax
import jax.numpy as jnp
from jax import lax
from jax.experimental import pallas as pl
from jax.experimental.pallas import tpu as pltpu

F32 = jnp.float32
BF16 = jnp.bfloat16

D_MODEL = 1024
SSD_HEAD_DIM = 64
SSD_INNER = 1024
SSD_HEADS = 16
SSD_GROUPS = 2
SSD_STATE = 64
SSD_CONV = 4
SSD_CHUNK = 128
SSD_CONV_DIM = SSD_INNER + 2 * SSD_GROUPS * SSD_STATE
SB_HEADS = 8
SB_HEAD_DIM = 64
SB_WIDTH = 512
SB_BLOCK = 128
XA_HEADS = 4
XA_HEAD_DIM = 128
XA_WIDTH = 512
D_FF = 2816
N_EXPERTS = 8
TOP_K = 2
EPS = 1e-6

LANES = 128
VMEM_LIMIT = 56 * 1024 * 1024
ROW_TILE = 512
PROJ_CHUNK = 512
FF_CHUNK = 256
MOE_TILE = 512


def _segments(widths):
    starts = [sum(widths[:i]) for i in range(len(widths))]
    return tuple(zip(starts, widths))


SEG_Z, SEG_XBC, SEG_DT, SEG_Q, SEG_K, SEG_V, SEG_GSSD, SEG_GSB = _segments(
    (SSD_INNER, SSD_CONV_DIM, LANES, SB_WIDTH, SB_WIDTH, SB_WIDTH, D_MODEL, D_MODEL))


def _cparams(sem):
    return pltpu.CompilerParams(dimension_semantics=sem, vmem_limit_bytes=VMEM_LIMIT)


def _rms(x, g):
    return x * lax.rsqrt(jnp.mean(x * x, axis=-1, keepdims=True) + EPS) * g


def _split2(x):
    hi = x.astype(BF16)
    lo = (x - hi.astype(F32)).astype(BF16)
    return hi, lo


def _dot(a, b):
    return jnp.dot(a, b, preferred_element_type=F32)


def _dot_nt(a, b):
    return lax.dot_general(a, b, (((1,), (1,)), ((), ())), preferred_element_type=F32)


def _dot_tn(a, b):
    return lax.dot_general(a, b, (((0,), (0,)), ((), ())), preferred_element_type=F32)


def _dot_split(x_f32, m_bf16):
    hi, lo = _split2(x_f32)
    return _dot(hi, m_bf16) + _dot(lo, m_bf16)


def _silu(x):
    return x / (1.0 + jnp.exp(-x))


def _sigmoid(x):
    return 1.0 / (1.0 + jnp.exp(-x))


def _norm_proj_kernel(x_ref, g_ref, w_ref, *out_refs, segs):
    x = x_ref[...]
    u = _rms(x, g_ref[...]).astype(BF16)
    for (start, width), o_ref in zip(segs, out_refs):
        for c0 in range(0, width, PROJ_CHUNK):
            cw = min(PROJ_CHUNK, width - c0)
            r = _dot(u, w_ref[:, start + c0:start + c0 + cw])
            o_ref[:, c0:c0 + cw] = r.astype(o_ref.dtype)


def norm_proj(x, g, w, segs, dtypes, name):
    m, k = x.shape
    tm = min(ROW_TILE, m)
    n = w.shape[1]
    out_shape = [jax.ShapeDtypeStruct((m, wd), dt) for (_, wd), dt in zip(segs, dtypes)]
    out_specs = [pl.BlockSpec((tm, wd), lambda i: (i, 0)) for (_, wd) in segs]
    return pl.pallas_call(
        functools.partial(_norm_proj_kernel, segs=tuple(segs)),
        grid=(m // tm,),
        in_specs=[pl.BlockSpec((tm, k), lambda i: (i, 0)),
                  pl.BlockSpec((1, k), lambda i: (0, 0)),
                  pl.BlockSpec((k, n), lambda i: (0, 0))],
        out_specs=out_specs,
        out_shape=out_shape,
        compiler_params=_cparams(("parallel",)),
        name=name,
    )(x, g.reshape(1, k), w)


def _ssd_kernel(z_ref, xbc_ref, dt_ref, cw_ref, cb_ref, dtb_ref, aneg_ref, dsk_ref,
                ng_ref, e_ref, o_ref, halo_scr, state_scr, y_scr):
    L = SSD_CHUNK
    c = pl.program_id(1)

    @pl.when(c == 0)
    def _():
        halo_scr[...] = jnp.zeros_like(halo_scr)
        state_scr[...] = jnp.zeros_like(state_scr)

    x_cur = xbc_ref[0].astype(F32)
    halo = halo_scr[...]
    sub = lax.broadcasted_iota(jnp.int32, (8, SSD_CONV_DIM), 0)
    acc = cb_ref[...] + x_cur * cw_ref[SSD_CONV - 1:SSD_CONV, :]
    for d in range(1, SSD_CONV):
        rolled = pltpu.roll(x_cur, d, axis=0)
        head = jnp.where(sub < d, pltpu.roll(halo, d, axis=0), rolled[0:8, :])
        shifted = jnp.concatenate([head, rolled[8:, :]], axis=0)
        acc = acc + shifted * cw_ref[SSD_CONV - 1 - d:SSD_CONV - d, :]
    halo_scr[...] = x_cur[L - 8:L, :]
    xc = _silu(acc)
    xs = xc[:, :SSD_INNER]
    b128 = xc[:, SSD_INNER:SSD_INNER + LANES].astype(BF16)
    c128 = xc[:, SSD_INNER + LANES:SSD_INNER + 2 * LANES]

    row = lax.broadcasted_iota(jnp.int32, (L, L), 0)
    col = lax.broadcasted_iota(jnp.int32, (L, L), 1)
    causal = col <= row
    tril = jnp.where(causal, 1.0, 0.0).astype(BF16)

    dt_in = dt_ref[0] + dtb_ref[...]
    dt = jnp.maximum(dt_in, 0.0) + jnp.log(1.0 + jnp.exp(-jnp.abs(dt_in)))
    a = dt * aneg_ref[...]
    a_hi = a.astype(BF16)
    a_mid = (a - a_hi.astype(F32))
    a_mid_b = a_mid.astype(BF16)
    a_lo = (a_mid - a_mid_b.astype(F32)).astype(BF16)
    acum = _dot(tril, a_hi) + _dot(tril, a_mid_b) + _dot(tril, a_lo)
    acum_t = acum.T
    alast = acum[L - 1:L, :]

    expand = e_ref[...]
    ea_e = _dot_split(jnp.exp(acum), expand)
    w_e = _dot_split(dt * jnp.exp(alast - acum), expand)
    cd_e = _dot_split(jnp.broadcast_to(jnp.exp(alast), (8, LANES)), expand)[0:1, :]
    dt_t = dt.T

    x_b = xs.astype(BF16)
    xw_b = (xs * w_e).astype(BF16)

    lane_half = lax.broadcasted_iota(jnp.int32, (L, LANES), 1) < SSD_STATE
    zero_b = jnp.zeros((L, LANES), BF16)
    c128_b = c128.astype(BF16)
    cb_g = [
        _dot_nt(jnp.where(lane_half, c128_b, zero_b), b128),
        _dot_nt(jnp.where(lane_half, zero_b, c128_b), b128),
    ]
    heads_per_group = SSD_HEADS // SSD_GROUPS
    for p in range(SSD_HEADS // 2):
        xp = x_b[:, p * LANES:(p + 1) * LANES]
        y_pair = None
        for s in range(2):
            h = 2 * p + s
            seg = acum[:, h:h + 1] - acum_t[h:h + 1, :]
            decay = jnp.exp(jnp.where(causal, seg, -jnp.inf))
            m_h = (cb_g[h // heads_per_group] * decay * dt_t[h:h + 1, :]).astype(BF16)
            x_h = jnp.where(lane_half, xp, zero_b) if s == 0 else jnp.where(lane_half, zero_b, xp)
            t = _dot(m_h, x_h)
            y_pair = t if y_pair is None else y_pair + t
        y_scr[:, p * LANES:(p + 1) * LANES] = y_pair

    state = state_scr[...]
    y_off = _dot(c128_b, state.astype(BF16)) * ea_e
    s_new = _dot_tn(b128, xw_b)
    srow = lax.broadcasted_iota(jnp.int32, (LANES, SSD_INNER), 0) < SSD_STATE
    scol = lax.broadcasted_iota(jnp.int32, (LANES, SSD_INNER), 1) < SSD_INNER // SSD_GROUPS
    state_scr[...] = state * cd_e + jnp.where(srow == scol, s_new, 0.0)

    y = y_scr[...] + y_off + xs * dsk_ref[...]
    y = y * _silu(z_ref[0].astype(F32))
    gw = SSD_INNER // SSD_GROUPS
    for g in range(SSD_GROUPS):
        yg = y[:, g * gw:(g + 1) * gw]
        o_ref[0, :, g * gw:(g + 1) * gw] = _rms(yg, ng_ref[:, g * gw:(g + 1) * gw]).astype(o_ref.dtype)


def ssd_branch(z, xbc, dt, conv_w, conv_b, dt_bias, a_log, d_skip, norm_g):
    bsz, s, _ = z.shape
    nc = s // SSD_CHUNK
    pad = LANES - SSD_HEADS
    dtb = jnp.pad(dt_bias.astype(F32), (0, pad)).reshape(1, LANES)
    aneg = jnp.pad(-jnp.exp(a_log.astype(F32)), (0, pad)).reshape(1, LANES)
    dsk = jnp.repeat(d_skip.astype(F32), SSD_HEAD_DIM).reshape(1, SSD_INNER)
    expand = (jnp.arange(LANES)[:, None] == (jnp.arange(SSD_INNER)[None, :] // SSD_HEAD_DIM)).astype(BF16)
    const = lambda shape: pl.BlockSpec(shape, lambda b, c: (0,) * len(shape))
    return pl.pallas_call(
        _ssd_kernel,
        grid=(bsz, nc),
        in_specs=[pl.BlockSpec((1, SSD_CHUNK, SSD_INNER), lambda b, c: (b, c, 0)),
                  pl.BlockSpec((1, SSD_CHUNK, SSD_CONV_DIM), lambda b, c: (b, c, 0)),
                  pl.BlockSpec((1, SSD_CHUNK, LANES), lambda b, c: (b, c, 0)),
                  const((SSD_CONV, SSD_CONV_DIM)),
                  const((1, SSD_CONV_DIM)),
                  const((1, LANES)),
                  const((1, LANES)),
                  const((1, SSD_INNER)),
                  const((1, SSD_INNER)),
                  const((LANES, SSD_INNER))],
        out_specs=pl.BlockSpec((1, SSD_CHUNK, SSD_INNER), lambda b, c: (b, c, 0)),
        out_shape=jax.ShapeDtypeStruct((bsz, s, SSD_INNER), BF16),
        scratch_shapes=[pltpu.VMEM((8, SSD_CONV_DIM), F32),
                        pltpu.VMEM((LANES, SSD_INNER), F32),
                        pltpu.VMEM((SSD_CHUNK, SSD_INNER), F32)],
        compiler_params=_cparams(("parallel", "arbitrary")),
        name="ssd_scan",
    )(z, xbc, dt, conv_w.astype(F32), conv_b.astype(F32).reshape(1, -1), dtb, aneg, dsk,
      norm_g.astype(F32).reshape(1, -1), expand)


SB_PAIRS = SB_WIDTH // LANES
EXP_UNDERFLOW = -104.0


def _sb_kernel(q_ref, k_ref, v_ref, nsuffix_ref, o_ref, run_scr, acc_scr, z_scr):
    T = SB_BLOCK
    qi = pl.program_id(1)
    nsuffix = nsuffix_ref[...]
    rw = lax.broadcasted_iota(jnp.int32, (T, 2 * T), 0)
    cw = lax.broadcasted_iota(jnp.int32, (T, 2 * T), 1)
    strict_w = (cw & (T - 1)) < rw
    strict = strict_w[:, :T]
    q_all = q_ref[0] * jnp.asarray(SB_HEAD_DIM ** -0.5, BF16)
    pairs = range(SB_PAIRS)
    lane_lo = lax.broadcasted_iota(jnp.int32, (T, LANES), 1) < SB_HEAD_DIM
    zero_b = jnp.zeros((T, LANES), BF16)

    run_scr[...] = jnp.zeros_like(run_scr)
    acc_scr[...] = jnp.zeros_like(acc_scr)

    def pair_rows(ref, j, p):
        x = ref[0, pl.ds(pl.multiple_of(j * T, T), T), p * LANES:(p + 1) * LANES]
        return jnp.concatenate([jnp.where(lane_lo, x, zero_b), jnp.where(lane_lo, zero_b, x)], axis=0)

    def scores(j):
        for p in pairs:
            z_scr[p] = _dot_nt(q_all[:, p * LANES:(p + 1) * LANES], pair_rows(k_ref, j, p))

    scores(qi)

    def block(j, diag):
        runmax = None
        sps, lsigs = [], []
        for p in pairs:
            z = z_scr[p]
            sp = jnp.maximum(z, 0.0) + jnp.log(1.0 + jnp.exp(-jnp.abs(z)))
            lsigs.append(z - sp)
            sps.append(jnp.where(strict_w, sp, 0.0) if diag else sp)
        lbs = [_dot(sps[p].astype(BF16), nsuffix) for p in pairs]
        scores(jnp.maximum(j - 1, 0))
        for p in pairs:
            args = lsigs[p] + lbs[p]
            atts = []
            for h in range(2):
                run = run_scr[2 * p + h]
                att = jnp.exp(args[:, h * T:(h + 1) * T] + run)
                if diag:
                    att = jnp.where(strict, att, 0.0)
                atts.append(att.astype(BF16))
                run_new = run[:, 0:1] - jnp.sum(sps[p][:, h * T:(h + 1) * T], axis=-1, keepdims=True)
                run_scr[2 * p + h] = jnp.broadcast_to(run_new, (T, LANES))
                runmax = run_new if runmax is None else jnp.maximum(runmax, run_new)
            acc_scr[p] += _dot(jnp.concatenate(atts, axis=1), pair_rows(v_ref, j, p))
        return (jnp.max(runmax) > EXP_UNDERFLOW).astype(jnp.int32)

    go = block(qi, True)

    def body(carry):
        i, _ = carry
        return i + 1, block(qi - i, False)

    lax.while_loop(lambda c: (c[0] <= qi) & (c[1] > 0), body, (jnp.int32(1), go))
    for p in range(SB_PAIRS):
        o_ref[0, :, p * LANES:(p + 1) * LANES] = acc_scr[p].astype(o_ref.dtype)


def stick_breaking(q, k, v):
    bsz, s, _ = q.shape
    nq = s // SB_BLOCK
    idx = jnp.arange(2 * SB_BLOCK)
    same_head = (idx[:, None] < SB_BLOCK) == (idx[None, :] < SB_BLOCK)
    nsuffix = jnp.where(same_head & (idx[:, None] > idx[None, :]), -1.0, 0.0).astype(BF16)
    return pl.pallas_call(
        _sb_kernel,
        grid=(bsz, nq),
        in_specs=[pl.BlockSpec((1, SB_BLOCK, SB_WIDTH), lambda b, i: (b, i, 0)),
                  pl.BlockSpec((1, s, SB_WIDTH), lambda b, i: (b, 0, 0)),
                  pl.BlockSpec((1, s, SB_WIDTH), lambda b, i: (b, 0, 0)),
                  pl.BlockSpec((2 * SB_BLOCK, 2 * SB_BLOCK), lambda b, i: (0, 0))],
        out_specs=pl.BlockSpec((1, SB_BLOCK, SB_WIDTH), lambda b, i: (b, i, 0)),
        out_shape=jax.ShapeDtypeStruct((bsz, s, SB_WIDTH), BF16),
        scratch_shapes=[pltpu.VMEM((SB_HEADS, SB_BLOCK, LANES), F32),
                        pltpu.VMEM((SB_PAIRS, SB_BLOCK, LANES), F32),
                        pltpu.VMEM((SB_PAIRS, SB_BLOCK, 2 * SB_BLOCK), F32)],
        compiler_params=_cparams(("parallel", "arbitrary")),
        name="stick_breaking",
    )(q, k, v, nsuffix)


def _xattn_kernel(h_ref, ys_ref, yb_ref, gs_ref, gb_ref, ws_ref, wb_ref, wout_ref, g_ref, wq_ref, kv_ref, wo_ref,
                  o_ref):
    ms = _sigmoid(gs_ref[0].astype(F32)) * _dot(ys_ref[0], ws_ref[...])
    mb = _sigmoid(gb_ref[0].astype(F32)) * _dot(yb_ref[0], wb_ref[...])
    h = h_ref[0] + _dot((ms + mb).astype(BF16), wout_ref[...])
    u = _rms(h, g_ref[...]).astype(BF16)
    q = _dot(u, wq_ref[...]).astype(BF16)
    scale = XA_HEAD_DIM ** -0.5
    outs = []
    for hd in range(XA_HEADS):
        lo = hd * XA_HEAD_DIM
        kh = kv_ref[0, :, lo:lo + XA_HEAD_DIM]
        vh = kv_ref[0, :, XA_WIDTH + lo:XA_WIDTH + lo + XA_HEAD_DIM]
        s = _dot_nt(q[:, lo:lo + XA_HEAD_DIM], kh) * scale
        e = jnp.exp(s - jnp.max(s, axis=-1, keepdims=True))
        oh = _dot(e.astype(BF16), vh) / jnp.sum(e, axis=-1, keepdims=True)
        outs.append(oh.astype(BF16))
    o = jnp.concatenate(outs, axis=-1)
    o_ref[0] = h + _dot(o, wo_ref[...])


def mix_and_cross_attention(h, y_ssd, y_sb, g_ssd, g_sb, w_ssd_o, w_sb_o, w_out, g, wq, kv, wo):
    bsz, s, d = h.shape
    tm = min(ROW_TILE, s)
    mem_len = kv.shape[1]
    rowspec = lambda w: pl.BlockSpec((1, tm, w), lambda b, i: (b, i, 0))
    wspec = lambda r, c: pl.BlockSpec((r, c), lambda b, i: (0, 0))
    return pl.pallas_call(
        _xattn_kernel,
        grid=(bsz, s // tm),
        in_specs=[rowspec(d), rowspec(SSD_INNER), rowspec(SB_WIDTH), rowspec(d), rowspec(d),
                  wspec(SSD_INNER, d), wspec(SB_WIDTH, d), wspec(d, d),
                  pl.BlockSpec((1, d), lambda b, i: (0, 0)),
                  pl.BlockSpec((d, XA_WIDTH), lambda b, i: (0, 0)),
                  pl.BlockSpec((1, mem_len, 2 * XA_WIDTH), lambda b, i: (b, 0, 0)),
                  pl.BlockSpec((XA_WIDTH, d), lambda b, i: (0, 0))],
        out_specs=pl.BlockSpec((1, tm, d), lambda b, i: (b, i, 0)),
        out_shape=jax.ShapeDtypeStruct((bsz, s, d), F32),
        compiler_params=_cparams(("parallel", "parallel")),
        name="mix_cross_attention",
    )(h, y_ssd, y_sb, g_ssd, g_sb, w_ssd_o, w_sb_o, w_out, g.reshape(1, d), wq, kv, wo)


def _swiglu_tile(u, wg_ref, wu_ref, wd_ref):
    acc = None
    for c0 in range(0, D_FF, FF_CHUNK):
        gt = _dot(u, wg_ref[0, :, c0:c0 + FF_CHUNK])
        up = _dot(u, wu_ref[0, :, c0:c0 + FF_CHUNK])
        hid = (_silu(gt) * up).astype(BF16)
        t = _dot(hid, wd_ref[0, c0:c0 + FF_CHUNK, :])
        acc = t if acc is None else acc + t
    return acc


def _dense_ffn_kernel(h_ref, g_ref, wg_ref, wu_ref, wd_ref, o_ref):
    h = h_ref[...]
    u = _rms(h, g_ref[...]).astype(BF16)
    o_ref[...] = h + _swiglu_tile(u, wg_ref, wu_ref, wd_ref)


def dense_ffn(h, g, wg, wu, wd, layer):
    m, d = h.shape
    tm = min(ROW_TILE, m)
    return pl.pallas_call(
        _dense_ffn_kernel,
        grid=(m // tm,),
        in_specs=[pl.BlockSpec((tm, d), lambda i: (i, 0)),
                  pl.BlockSpec((1, d), lambda i: (0, 0)),
                  pl.BlockSpec((1, d, D_FF), lambda i: (layer, 0, 0)),
                  pl.BlockSpec((1, d, D_FF), lambda i: (layer, 0, 0)),
                  pl.BlockSpec((1, D_FF, d), lambda i: (layer, 0, 0))],
        out_specs=pl.BlockSpec((tm, d), lambda i: (i, 0)),
        out_shape=jax.ShapeDtypeStruct((m, d), F32),
        compiler_params=_cparams(("parallel",)),
        name="dense_ffn",
    )(h, g.reshape(1, d), wg, wu, wd)


def _expert_ffn_kernel(te_ref, nt_ref, x_ref, wg_ref, wu_ref, wd_ref, o_ref):
    @pl.when(pl.program_id(0) < nt_ref[0])
    def _():
        o_ref[...] = _swiglu_tile(x_ref[...], wg_ref, wu_ref, wd_ref).astype(o_ref.dtype)

    @pl.when(pl.program_id(0) >= nt_ref[0])
    def _():
        o_ref[...] = jnp.zeros_like(o_ref)


def expert_ffn(x_sorted, tile_expert, n_tiles_used, wg, wu, wd, tm):
    r, d = x_sorted.shape
    grid_spec = pltpu.PrefetchScalarGridSpec(
        num_scalar_prefetch=2,
        grid=(r // tm,),
        in_specs=[pl.BlockSpec((tm, d), lambda i, te, nt: (i, 0)),
                  pl.BlockSpec((1, d, D_FF), lambda i, te, nt: (te[i], 0, 0)),
                  pl.BlockSpec((1, d, D_FF), lambda i, te, nt: (te[i], 0, 0)),
                  pl.BlockSpec((1, D_FF, d), lambda i, te, nt: (te[i], 0, 0))],
        out_specs=pl.BlockSpec((tm, d), lambda i, te, nt: (i, 0)),
    )
    return pl.pallas_call(
        _expert_ffn_kernel,
        grid_spec=grid_spec,
        out_shape=jax.ShapeDtypeStruct((r, d), BF16),
        compiler_params=_cparams(("arbitrary",)),
        name="expert_ffn",
    )(tile_expert, n_tiles_used, x_sorted, wg, wu, wd)


def _router_kernel(h_ref, g_ref, wr_hi_ref, wr_lo_ref, earlier_ref, u_ref, route_ref, fields_ref, counts_ref):
    uf = _rms(h_ref[...], g_ref[...])
    u_hi = uf.astype(BF16)
    u_ref[...] = u_hi
    u_lo = (uf - u_hi.astype(F32)).astype(BF16)
    logits = _dot(u_hi, wr_hi_ref[...]) + (_dot(u_lo, wr_hi_ref[...]) + _dot(u_hi, wr_lo_ref[...]))
    lane = lax.broadcasted_iota(jnp.int32, logits.shape, 1)
    logits = jnp.where(lane < N_EXPERTS, logits, -jnp.inf)
    m1 = jnp.max(logits, axis=-1, keepdims=True)
    i1 = jnp.min(jnp.where(logits == m1, lane, LANES), axis=-1, keepdims=True)
    rest = jnp.where(lane == i1, -jnp.inf, logits)
    m2 = jnp.max(rest, axis=-1, keepdims=True)
    i2 = jnp.min(jnp.where(rest == m2, lane, LANES), axis=-1, keepdims=True)
    e2 = jnp.exp(m2 - m1)
    w1 = 1.0 / (1.0 + e2)
    w2 = e2 / (1.0 + e2)
    route = jnp.where(lane == 0, w1, jnp.where(lane == 1, w2, 0.0))
    route = jnp.where(lane == 2, i1.astype(F32), route)
    route = jnp.where(lane == 3, i2.astype(F32), route)
    tm = logits.shape[0]
    picked = jnp.where(lane == i1, 1.0, jnp.where(lane == i2, 1.0, 0.0))
    before = _dot(earlier_ref[...], picked.astype(BF16))
    rank1 = jnp.sum(jnp.where(lane == i1, before, 0.0), axis=-1, keepdims=True)
    rank2 = jnp.sum(jnp.where(lane == i2, before, 0.0), axis=-1, keepdims=True)
    route = jnp.where(lane == 4, rank1, route)
    route = jnp.where(lane == 5, rank2, route)
    route_ref[...] = route
    for b in range(tm // LANES):
        fields_ref[:, b * LANES:(b + 1) * LANES] = route[b * LANES:(b + 1) * LANES, :].T[0:8, :]
    counts_ref[...] = jnp.broadcast_to(jnp.sum(picked, axis=0, keepdims=True), counts_ref.shape)


def router(h, g, w_router):
    m, d = h.shape
    tm = min(ROW_TILE, m)
    nt = m // tm
    wr = jnp.pad(w_router.astype(F32), ((0, 0), (0, LANES - N_EXPERTS)))
    wr_hi = wr.astype(BF16)
    wr_lo = (wr - wr_hi.astype(F32)).astype(BF16)
    earlier = (jnp.arange(tm)[None, :] < jnp.arange(tm)[:, None]).astype(BF16)
    return pl.pallas_call(
        _router_kernel,
        grid=(nt,),
        in_specs=[pl.BlockSpec((tm, d), lambda i: (i, 0)),
                  pl.BlockSpec((1, d), lambda i: (0, 0)),
                  pl.BlockSpec((d, LANES), lambda i: (0, 0)),
                  pl.BlockSpec((d, LANES), lambda i: (0, 0)),
                  pl.BlockSpec((tm, tm), lambda i: (0, 0))],
        out_specs=[pl.BlockSpec((tm, d), lambda i: (i, 0)),
                   pl.BlockSpec((tm, LANES), lambda i: (i, 0)),
                   pl.BlockSpec((8, tm), lambda i: (0, i)),
                   pl.BlockSpec((8, LANES), lambda i: (i, 0))],
        out_shape=[jax.ShapeDtypeStruct((m, d), BF16),
                   jax.ShapeDtypeStruct((m, LANES), F32),
                   jax.ShapeDtypeStruct((8, m), F32),
                   jax.ShapeDtypeStruct((8 * nt, LANES), F32)],
        compiler_params=_cparams(("parallel",)),
        name="router",
    )(h, g.reshape(1, d), wr_hi, wr_lo, earlier)


GRAN = 16
LOC_ROWS = TOP_K * ROW_TILE + N_EXPERTS * GRAN
MAX_GRAN = LOC_ROWS // GRAN


def _dispatch_kernel(gdst_ref, ngran_ref, ls_ref, zdst_ref, nused_ref, u_ref, f_ref, xout_ref, xs_scr, zero_scr,
                     sem):
    t = pl.program_id(0)
    nt = pl.num_programs(0)
    tm = u_ref.shape[0]
    f = f_ref[...]

    def zero_piece(k):
        dst = xout_ref.at[pl.ds(pl.multiple_of(jnp.maximum(zdst_ref[k], 0), GRAN), GRAN), :]
        return pltpu.make_async_copy(zero_scr.at[0:GRAN, :], dst, sem.at[1])

    def zero_tile(i):
        dst = xout_ref.at[pl.ds(pl.multiple_of(i * tm, tm), tm), :]
        return pltpu.make_async_copy(zero_scr, dst, sem.at[1])

    n_row_tiles = xout_ref.shape[0] // tm

    def for_zero_pieces(fn):
        def body(k, c):
            @pl.when(zdst_ref[k] >= 0)
            def _():
                fn(zero_piece(k))
            return c
        lax.fori_loop(0, zdst_ref.shape[0], body, 0)

    @pl.when(t == 0)
    def _():
        zero_scr[...] = jnp.zeros_like(zero_scr)
        for_zero_pieces(lambda cp: cp.start())
        lax.fori_loop(nused_ref[0], n_row_tiles, lambda i, c: (zero_tile(i).start(), c)[1], 0)

    def local_row(expert_row, rank_row):
        base = jnp.zeros_like(rank_row)
        for e in range(N_EXPERTS):
            base = jnp.where(expert_row == float(e), ls_ref[t * N_EXPERTS + e].astype(F32), base)
        return (base + rank_row).astype(jnp.int32)

    lp1 = local_row(f[2:3, :], f[4:5, :])
    lp2 = local_row(f[3:4, :], f[5:6, :])
    r = lax.broadcasted_iota(jnp.int32, (LOC_ROWS, tm), 0)
    sel = jnp.where(r == lp1, 1.0, jnp.where(r == lp2, 1.0, 0.0)).astype(BF16)
    xs = _dot(sel, u_ref[...]).astype(BF16)

    def piece(tile, q):
        src = xs_scr.at[pl.ds(pl.multiple_of(q * GRAN, GRAN), GRAN), :]
        dst = xout_ref.at[pl.ds(pl.multiple_of(gdst_ref[tile * MAX_GRAN + q], GRAN), GRAN), :]
        return pltpu.make_async_copy(src, dst, sem.at[0])

    def wait_all(tile):
        lax.fori_loop(0, ngran_ref[tile], lambda q, c: (piece(tile, q).wait(), c)[1], 0)

    @pl.when(t > 0)
    def _():
        wait_all(t - 1)

    xs_scr[...] = xs
    lax.fori_loop(0, ngran_ref[t], lambda q, c: (piece(t, q).start(), c)[1], 0)

    @pl.when(t == nt - 1)
    def _():
        wait_all(t)
        for_zero_pieces(lambda cp: cp.wait())
        lax.fori_loop(nused_ref[0], n_row_tiles, lambda i, c: (zero_tile(i).wait(), c)[1], 0)


def dispatch(u, fields, gdst, ngran, seg_ls, zdst, n_used, n_rows):
    m, d = u.shape
    tm = min(ROW_TILE, m)
    grid_spec = pltpu.PrefetchScalarGridSpec(
        num_scalar_prefetch=5,
        grid=(m // tm,),
        in_specs=[pl.BlockSpec((tm, d), lambda i, *_: (i, 0)),
                  pl.BlockSpec((8, tm), lambda i, *_: (0, i))],
        out_specs=pl.BlockSpec(memory_space=pl.ANY),
        scratch_shapes=[pltpu.VMEM((LOC_ROWS, d), BF16), pltpu.VMEM((tm, d), BF16),
                        pltpu.SemaphoreType.DMA((2,))],
    )
    return pl.pallas_call(
        _dispatch_kernel,
        grid_spec=grid_spec,
        out_shape=jax.ShapeDtypeStruct((n_rows, d), BF16),
        compiler_params=_cparams(("arbitrary",)),
        name="moe_dispatch",
    )(gdst, ngran, seg_ls, zdst, n_used, u, fields)


def _combine_kernel(gdst_ref, ngran_ref, h_ref, route_ref, ls_ref, g_ref, y_ref, o_ref, y_scr, sem,
                    *, final_norm):
    t = pl.program_id(0)
    nt = pl.num_programs(0)
    tm = h_ref.shape[0]
    slot = lax.rem(t, 2)

    def piece(tile, q):
        buf = lax.rem(tile, 2)
        src = y_ref.at[pl.ds(pl.multiple_of(gdst_ref[tile * MAX_GRAN + q], GRAN), GRAN), :]
        dst = y_scr.at[buf, pl.ds(pl.multiple_of(q * GRAN, GRAN), GRAN), :]
        return pltpu.make_async_copy(src, dst, sem.at[buf])

    def fetch(tile):
        lax.fori_loop(0, ngran_ref[tile], lambda q, c: (piece(tile, q).start(), c)[1], 0)

    @pl.when(t == 0)
    def _():
        y_scr[...] = jnp.zeros_like(y_scr)
        fetch(t)

    @pl.when(t + 1 < nt)
    def _():
        fetch(t + 1)

    route = route_ref[...]
    lane = lax.broadcasted_iota(jnp.int32, (tm, LANES), 1)
    ls_row = ls_ref[0:1, :]

    def local_row(expert_col, rank_col):
        base = jnp.sum(jnp.where(lane == expert_col.astype(jnp.int32), ls_row, 0.0), axis=-1, keepdims=True)
        return (base + rank_col).astype(jnp.int32)

    c = lax.broadcasted_iota(jnp.int32, (tm, LOC_ROWS), 1)
    sel1 = jnp.where(c == local_row(route[:, 2:3], route[:, 4:5]), 1.0, 0.0).astype(BF16)
    sel2 = jnp.where(c == local_row(route[:, 3:4], route[:, 5:6]), 1.0, 0.0).astype(BF16)

    lax.fori_loop(0, ngran_ref[t], lambda q, c_: (piece(t, q).wait(), c_)[1], 0)
    y = y_scr[slot]
    out = h_ref[...] + (route[:, 0:1] * _dot(sel1, y) + route[:, 1:2] * _dot(sel2, y))
    if final_norm:
        out = _rms(out, g_ref[...])
    o_ref[...] = out


def combine(h, y_sorted, route, ls_rows, gdst, ngran, g, final_norm):
    m, d = h.shape
    tm = min(ROW_TILE, m)
    grid_spec = pltpu.PrefetchScalarGridSpec(
        num_scalar_prefetch=2,
        grid=(m // tm,),
        in_specs=[pl.BlockSpec((tm, d), lambda i, *_: (i, 0)),
                  pl.BlockSpec((tm, LANES), lambda i, *_: (i, 0)),
                  pl.BlockSpec((8, LANES), lambda i, *_: (i, 0)),
                  pl.BlockSpec((1, d), lambda i, *_: (0, 0)),
                  pl.BlockSpec(memory_space=pl.ANY)],
        out_specs=pl.BlockSpec((tm, d), lambda i, *_: (i, 0)),
        scratch_shapes=[pltpu.VMEM((2, LOC_ROWS, d), BF16), pltpu.SemaphoreType.DMA((2,))],
    )
    return pl.pallas_call(
        functools.partial(_combine_kernel, final_norm=final_norm),
        grid_spec=grid_spec,
        out_shape=jax.ShapeDtypeStruct((m, d), F32),
        compiler_params=_cparams(("arbitrary",)),
        name="moe_combine",
    )(gdst, ngran, h, route, ls_rows, g.reshape(1, d), y_sorted)


def _final_norm_kernel(h_ref, g_ref, o_ref):
    o_ref[...] = _rms(h_ref[...], g_ref[...])


def final_norm_rows(h, g):
    m, d = h.shape
    tm = min(ROW_TILE, m)
    return pl.pallas_call(
        _final_norm_kernel,
        grid=(m // tm,),
        in_specs=[pl.BlockSpec((tm, d), lambda i: (i, 0)), pl.BlockSpec((1, d), lambda i: (0, 0))],
        out_specs=pl.BlockSpec((tm, d), lambda i: (i, 0)),
        out_shape=jax.ShapeDtypeStruct((m, d), F32),
        compiler_params=_cparams(("parallel",)),
        name="final_norm",
    )(h, g.reshape(1, d))


def moe_ffn(h, g, w_router, wg, wu, wd, layer, g_final, apply_final_norm):
    m, d = h.shape
    tm = min(MOE_TILE, m)
    nt = m // tm
    u, route, fields, counts = router(h, g, w_router)
    counts = counts.reshape(nt, 8, LANES)[:, 0, :N_EXPERTS].astype(jnp.int32)
    seg = (counts + GRAN - 1) // GRAN * GRAN
    loc_end = jnp.cumsum(seg, axis=1)
    loc_start = loc_end - seg
    group_tiles = (jnp.sum(seg, axis=0) + tm - 1) // tm
    tile_end = jnp.cumsum(group_tiles)
    group_start = (tile_end - group_tiles) * tm
    seg_dst = group_start[None, :] + jnp.cumsum(seg, axis=0) - seg
    q = jnp.arange(MAX_GRAN, dtype=jnp.int32) * GRAN
    piece_expert = jnp.minimum(
        jnp.sum((q[None, :, None] >= loc_end[:, None, :]).astype(jnp.int32), axis=-1), N_EXPERTS - 1)
    owner = piece_expert[:, :, None] == jnp.arange(N_EXPERTS)[None, None, :]
    gdst = (jnp.sum(jnp.where(owner, (seg_dst - loc_start)[:, None, :], 0), axis=-1)
            + q[None, :]).reshape(-1).astype(jnp.int32)
    ngran = (loc_end[:, -1] // GRAN).astype(jnp.int32)
    n_rows = -(-(TOP_K * m + nt * N_EXPERTS * (GRAN - 1) + N_EXPERTS * (tm - 1)) // tm) * tm
    n_tiles = n_rows // tm
    tile_expert = jnp.minimum(
        jnp.sum((jnp.arange(n_tiles)[:, None] >= tile_end[None, :]).astype(jnp.int32), axis=1),
        N_EXPERTS - 1).astype(jnp.int32) + layer * N_EXPERTS
    n_used = tile_end[-1:].astype(jnp.int32)
    ls_rows = jnp.broadcast_to(
        jnp.pad(loc_start.astype(F32), ((0, 0), (0, LANES - N_EXPERTS)))[:, None, :], (nt, 8, LANES)
    ).reshape(nt * 8, LANES)
    group_rows = jnp.sum(seg, axis=0)
    zk = jnp.arange(tm // GRAN - 1, dtype=jnp.int32)[None, :] * GRAN
    zdst = jnp.where(zk < (group_tiles * tm - group_rows)[:, None],
                     (group_start + group_rows)[:, None] + zk, -1).reshape(-1).astype(jnp.int32)
    x_sorted = dispatch(u, fields, gdst, ngran, loc_start.reshape(-1).astype(jnp.int32), zdst, n_used, n_rows)
    y = expert_ffn(x_sorted, tile_expert, n_used, wg, wu, wd, tm)
    return combine(h, y, route, ls_rows, gdst, ngran, g_final, apply_final_norm)


def _cast_kernel(x_ref, o_ref):
    o_ref[...] = x_ref[...].astype(o_ref.dtype)


CAST_BLOCK_BYTES = 6 * 1024 * 1024


def cast_bf16(w):
    rows, cols = w.shape[-2:]
    lead = w.size // (rows * cols)
    tr = next(rows // k for k in range(1, rows // 16 + 1)
              if rows % k == 0 and (rows // k) % 16 == 0 and (rows // k) * cols * 4 <= CAST_BLOCK_BYTES)
    out = pl.pallas_call(
        _cast_kernel,
        grid=(lead, rows // tr),
        in_specs=[pl.BlockSpec((1, tr, cols), lambda l, i: (l, i, 0))],
        out_specs=pl.BlockSpec((1, tr, cols), lambda l, i: (l, i, 0)),
        out_shape=jax.ShapeDtypeStruct((lead, rows, cols), BF16),
        compiler_params=_cparams(("parallel", "parallel")),
        name="cast_bf16",
    )(w.reshape(lead, rows, cols))
    return out.reshape(w.shape)


def _in_proj_weight(w):
    o_dt = SSD_INNER + SSD_CONV_DIM
    o_qkv = o_dt + SSD_HEADS
    dt_cols = jnp.pad(w[:, o_dt:o_qkv], ((0, 0), (0, LANES - SSD_HEADS)))
    return jnp.concatenate([w[:, :o_dt], dt_cols, w[:, o_qkv:]], axis=1)


def kernel(x, mem, norm_mix, w_in, conv_w, conv_b, dt_bias, a_log, d_skip, ssd_norm, w_ssd_o, w_sb_o, w_out, norm_xa, norm_mem, xa_wq, xa_wk, xa_wv, xa_wo, norm_ffn, ffn_w_gate, ffn_w_up, ffn_w_down, moe_router, moe_w_gate, moe_w_up, moe_w_down, final_norm):
    bsz, s, d = x.shape
    m = bsz * s
    depth = w_in.shape[0]
    mem2 = mem.reshape(-1, d)
    h = x.reshape(m, d)
    segs = (SEG_Z, SEG_XBC, SEG_DT, SEG_Q, SEG_K, SEG_V, SEG_GSSD, SEG_GSB)
    seg_dtypes = (BF16, BF16, F32, BF16, BF16, BF16, BF16, BF16)
    w_in, w_ssd_o, w_sb_o, w_out, xa_wq, xa_wk, xa_wv, xa_wo = [
        cast_bf16(w) for w in (w_in, w_ssd_o, w_sb_o, w_out, xa_wq, xa_wk, xa_wv, xa_wo)]
    ffn_w = [cast_bf16(w) for w in (ffn_w_gate, ffn_w_up, ffn_w_down)]
    moe_w = [cast_bf16(w).reshape((-1,) + w.shape[2:]) for w in (moe_w_gate, moe_w_up, moe_w_down)]
    for i in range(depth):
        z, xbc, dt, q, k, v, g_ssd, g_sb = norm_proj(
            h, norm_mix[i], _in_proj_weight(w_in[i]), segs, seg_dtypes, "in_proj")
        sh = lambda a: a.reshape(bsz, s, a.shape[-1])
        y_ssd = ssd_branch(sh(z), sh(xbc), sh(dt), conv_w[i], conv_b[i], dt_bias[i], a_log[i],
                           d_skip[i], ssd_norm[i])
        y_sb = stick_breaking(sh(q), sh(k), sh(v))
        w_kv = jnp.concatenate([xa_wk[i], xa_wv[i]], axis=1)
        (kv,) = norm_proj(mem2, norm_mem[i], w_kv, ((0, 2 * XA_WIDTH),), (BF16,), "mem_kv")
        h = mix_and_cross_attention(sh(h), y_ssd, y_sb, sh(g_ssd), sh(g_sb), w_ssd_o[i], w_sb_o[i], w_out[i],
                                    norm_xa[i], xa_wq[i], kv.reshape(bsz, -1, 2 * XA_WIDTH),
                                    xa_wo[i]).reshape(m, d)
        j = i // 2
        last = i == depth - 1
        if i % 2 == 0:
            h = dense_ffn(h, norm_ffn[i], *ffn_w, j)
            if last:
                h = final_norm_rows(h, final_norm)
        else:
            h = moe_ffn(h, norm_ffn[i], moe_router[j], *moe_w, j, final_norm, last)
    return h.reshape(bsz, s, d)
```
